```python
import math
import jax, jax.numpy as jnp
from jax import lax
import numpy as np

D_MODEL = 4096
BATCH = 4
SEQ = 4096
DEPTH = 4

GRID_W = 64
CTX_LEN = 256
N_MIXERS = 3
HEAD_DIM = 128
A_HEADS = D_MODEL // HEAD_DIM
A_KV_HEADS = A_HEADS // 4
B_HEADS = D_MODEL // HEAD_DIM
B_WIN_R = 8
B_WIN_C = 16
B_COL_BLOCK = 16
B_KEY_COLS = 32
C_HEADS = D_MODEL // (2 * HEAD_DIM)
MLP_HIDDEN = 4 * D_MODEL
MOD_RANK = 1024
N_MOD = 6
Q_BLOCK = 128
ROPE_THETA = 10000.0
EPS = 1e-6
N_A_LAYERS = (DEPTH + N_MIXERS - 1) // N_MIXERS
N_B_LAYERS = (DEPTH + N_MIXERS - 2) // N_MIXERS
N_C_LAYERS = DEPTH // N_MIXERS

kernel_name = "hybrid_dit_gqa_natten_diffattn"


def rms_norm(x, gain):
    x32 = x.astype(jnp.float32)
    y = x32 * lax.rsqrt(jnp.mean(x32 * x32, axis=-1, keepdims=True) + EPS)
    return y.astype(x.dtype) * gain


def modulate(u, shift, scale):
    return u * (1 + scale) + shift


def axial_rope_tables(n_tokens, dim):
    t = jnp.arange(n_tokens, dtype=jnp.int32)
    row = (t // GRID_W).astype(jnp.float32)
    col = (t % GRID_W).astype(jnp.float32)
    n_freq = dim // 4
    inv_freq = ROPE_THETA ** (-jnp.arange(n_freq, dtype=jnp.float32) / n_freq)
    ang = jnp.concatenate([row[:, None] * inv_freq, col[:, None] * inv_freq], axis=-1)
    return jnp.cos(ang), jnp.sin(ang)


def apply_rope(x, cos, sin):
    half = x.shape[-1] // 2
    bshape = (1, x.shape[1]) + (1,) * (x.ndim - 3) + (half,)
    c = cos.reshape(bshape).astype(x.dtype)
    s = sin.reshape(bshape).astype(x.dtype)
    xp = x.reshape(*x.shape[:-1], half, 2)
    x0, x1 = xp[..., 0], xp[..., 1]
    return jnp.stack([x0 * c - x1 * s, x0 * s + x1 * c], axis=-1).reshape(x.shape)


def sweep_query_blocks(block_fn, *qs):
    b, s = qs[0].shape[:2]
    nb = s // Q_BLOCK
    blocks = tuple(jnp.moveaxis(q.reshape(b, nb, Q_BLOCK, *q.shape[2:]), 1, 0) for q in qs)
    out = lax.map(lambda qb: block_fn(*qb), blocks)
    return jnp.moveaxis(out, 0, 1).reshape(b, s, *out.shape[3:])


def gqa_attend(q, k, v):
    s = jnp.einsum('bqhgd,bkhd->bhgqk', q, k, preferred_element_type=jnp.float32) * (q.shape[-1] ** -0.5)
    p = jax.nn.softmax(s, axis=-1).astype(v.dtype)
    return jnp.einsum('bhgqk,bkhd->bqhgd', p, v)


def channel_mixer(u, w1, w2):
    return jnp.square(jax.nn.relu(u @ w1)) @ w2


def mixer_gqa(h_lat, h_ctx, w_qkv, w_o, q_gain, k_gain, cos, sin, with_ctx_out):
    n_q, n_kv = A_HEADS * HEAD_DIM, A_KV_HEADS * HEAD_DIM
    group = A_HEADS // A_KV_HEADS

    def project(h):
        y = h @ w_qkv
        bb, n = h.shape[:2]
        q = rms_norm(y[..., :n_q].reshape(bb, n, A_KV_HEADS, group, HEAD_DIM), q_gain)
        k = rms_norm(y[..., n_q:n_q + n_kv].reshape(bb, n, A_KV_HEADS, HEAD_DIM), k_gain)
        v = y[..., n_q + n_kv:].reshape(bb, n, A_KV_HEADS, HEAD_DIM)
        return q, k, v

    q_l, k_l, v_l = project(h_lat)
    q_c, k_c, v_c = project(h_ctx)
    q_l = apply_rope(q_l, cos, sin)
    k_l = apply_rope(k_l, cos, sin)
    keys = jnp.concatenate([k_c, k_l], axis=1)
    vals = jnp.concatenate([v_c, v_l], axis=1)
    o_l = sweep_query_blocks(lambda qb: gqa_attend(qb, keys, vals), q_l)
    out_l = o_l.reshape(h_lat.shape[0], h_lat.shape[1], n_q) @ w_o
    out_c = None
    if with_ctx_out:
        out_c = gqa_attend(q_c, k_c, v_c).reshape(h_ctx.shape[0], h_ctx.shape[1], n_q) @ w_o
    return out_l, out_c


def mixer_neighbourhood(h_lat, h_ctx, w_qkv, w_o, rpb, with_ctx_out):
    b, s, _ = h_lat.shape
    rows = s // GRID_W
    wr = min(B_WIN_R, rows)
    n_h = B_HEADS * HEAD_DIM
    scale = HEAD_DIM ** -0.5

    def project(h):
        y = h @ w_qkv
        bb, n = h.shape[:2]
        return tuple(y[..., i * n_h:(i + 1) * n_h].reshape(bb, n, B_HEADS, HEAD_DIM) for i in range(3))

    q_l, k_l, v_l = project(h_lat)
    q_c, k_c, v_c = project(h_ctx)
    grid = (b, rows, GRID_W, B_HEADS, HEAD_DIM)
    qg, kg, vg = q_l.reshape(grid), k_l.reshape(grid), v_l.reshape(grid)
    n_cb = GRID_W // B_COL_BLOCK
    n_win = wr * B_KEY_COLS
    q_col_off = jnp.arange(B_COL_BLOCK, dtype=jnp.int32)
    key_col_off = jnp.arange(B_KEY_COLS, dtype=jnp.int32)
    key_row_off = jnp.arange(wr, dtype=jnp.int32)

    def block(idx):
        r = idx // n_cb
        j = idx % n_cb
        r0 = jnp.clip(r - B_WIN_R // 2, 0, rows - wr)
        qc = j * B_COL_BLOCK + q_col_off
        c0 = jnp.clip(qc - B_WIN_C // 2, 0, GRID_W - B_WIN_C)
        cs = jnp.clip(j * B_COL_BLOCK - (B_KEY_COLS - B_COL_BLOCK) // 2, 0, GRID_W - B_KEY_COLS)
        qb = lax.dynamic_slice(qg, (0, r, j * B_COL_BLOCK, 0, 0), (b, 1, B_COL_BLOCK, B_HEADS, HEAD_DIM))[:, 0]
        kb = lax.dynamic_slice(kg, (0, r0, cs, 0, 0), (b, wr, B_KEY_COLS, B_HEADS, HEAD_DIM))
        vb = lax.dynamic_slice(vg, (0, r0, cs, 0, 0), (b, wr, B_KEY_COLS, B_HEADS, HEAD_DIM))
        kc = cs + key_col_off
        kr = r0 + key_row_off
        in_win = (kc[None, :] >= c0[:, None]) & (kc[None, :] < c0[:, None] + B_WIN_C)
        dr = kr - r + (B_WIN_R - 1)
        dc = jnp.clip(kc[None, :] - qc[:, None] + (B_WIN_C - 1), 0, 2 * B_WIN_C - 2)
        bias = rpb[:, dr[None, :, None], dc[:, None, :]]
        s_win = jnp.einsum('bqhd,brkhd->bhqrk', qb, kb, preferred_element_type=jnp.float32) * scale + bias
        s_win = jnp.where(in_win[:, None, :], s_win, -jnp.inf).reshape(b, B_HEADS, B_COL_BLOCK, n_win)
        s_ctx = jnp.einsum('bqhd,bkhd->bhqk', qb, k_c, preferred_element_type=jnp.float32) * scale
        p = jax.nn.softmax(jnp.concatenate([s_win, s_ctx], axis=-1), axis=-1).astype(v_l.dtype)
        o = jnp.einsum('bhqk,bkhd->bqhd', p[..., :n_win], vb.reshape(b, n_win, B_HEADS, HEAD_DIM))
        return o + jnp.einsum('bhqk,bkhd->bqhd', p[..., n_win:], v_c)

    out = lax.map(block, jnp.arange(rows * n_cb, dtype=jnp.int32))
    out = out.reshape(rows, n_cb, b, B_COL_BLOCK, n_h).transpose(2, 0, 1, 3, 4).reshape(b, s, n_h)
    out_l = out @ w_o
    out_c = None
    if with_ctx_out:
        o_c = gqa_attend(q_c[:, :, :, None], k_c, v_c)[:, :, :, 0]
        out_c = o_c.reshape(h_ctx.shape[0], h_ctx.shape[1], n_h) @ w_o
    return out_l, out_c


def mixer_diff(h_lat, h_ctx, w_qkv, w_o, lq1, lk1, lq2, lk2, subln_g, cos, sin, lambda_init, with_ctx_out):
    n_qk = C_HEADS * 2 * HEAD_DIM
    scale = HEAD_DIM ** -0.5

    def project(h):
        y = h @ w_qkv
        bb, n = h.shape[:2]
        q = y[..., :n_qk].reshape(bb, n, C_HEADS, 2, HEAD_DIM)
        k = y[..., n_qk:2 * n_qk].reshape(bb, n, C_HEADS, 2, HEAD_DIM)
        v = y[..., 2 * n_qk:].reshape(bb, n, C_HEADS, 2 * HEAD_DIM)
        return q, k, v

    lam = (jnp.exp(jnp.sum(lq1.astype(jnp.float32) * lk1.astype(jnp.float32)))
           - jnp.exp(jnp.sum(lq2.astype(jnp.float32) * lk2.astype(jnp.float32))) + lambda_init)

    def attend(q, k, v):
        s = jnp.einsum('bqhcd,bkhcd->bchqk', q, k, preferred_element_type=jnp.float32) * scale
        p = jax.nn.softmax(s, axis=-1)
        a = (p[:, 0] - lam * p[:, 1]).astype(v.dtype)
        o = jnp.einsum('bhqk,bkhd->bqhd', a, v)
        return rms_norm(o, subln_g) * (1 - lambda_init)

    q_l, k_l, v_l = project(h_lat)
    q_c, k_c, v_c = project(h_ctx)
    q_l = apply_rope(q_l, cos, sin)
    k_l = apply_rope(k_l, cos, sin)
    keys = jnp.concatenate([k_c, k_l], axis=1)
    vals = jnp.concatenate([v_c, v_l], axis=1)
    o_l = sweep_query_blocks(lambda qb: attend(qb, keys, vals), q_l)
    out_l = o_l.reshape(h_lat.shape[0], h_lat.shape[1], D_MODEL) @ w_o
    out_c = None
    if with_ctx_out:
        out_c = attend(q_c, k_c, v_c).reshape(h_ctx.shape[0], h_ctx.shape[1], D_MODEL) @ w_o
    return out_l, out_c


def setup_inputs(seed: int = 0) -> dict:
    key = jax.random.key(seed)
    ks = jax.random.split(key, 32)
    counter = [0]

    def nrm(shape, scale):
        k = ks[counter[0]]
        counter[0] += 1
        return jax.random.normal(k, shape, jnp.float32) * scale

    d = D_MODEL
    qkv_a = (A_HEADS + 2 * A_KV_HEADS) * HEAD_DIM
    return {
        "x": nrm((BATCH, SEQ, d), 1.0),
        "c": nrm((BATCH, d), 1.0),
        "ctx": nrm((BATCH, CTX_LEN, d), 1.0),
        "c_ctx": nrm((d,), 1.0),
        "norm1_g": 1.0 + nrm((DEPTH, d), 0.02),
        "norm2_g": 1.0 + nrm((DEPTH, d), 0.02),
        "mod_down": nrm((DEPTH, d, MOD_RANK), d ** -0.5),
        "mod_up": nrm((DEPTH, MOD_RANK, N_MOD * d), 0.3 * MOD_RANK ** -0.5),
        "mod_b": nrm((DEPTH, N_MOD * d), 0.02),
        "mlp_w1": nrm((DEPTH, d, MLP_HIDDEN), d ** -0.5),
        "mlp_w2": nrm((DEPTH, MLP_HIDDEN, d), MLP_HIDDEN ** -0.5),
        "a_w_qkv": nrm((N_A_LAYERS, d, qkv_a), d ** -0.5),
        "a_w_o": nrm((N_A_LAYERS, A_HEADS * HEAD_DIM, d), (A_HEADS * HEAD_DIM) ** -0.5),
        "a_q_g": 1.0 + nrm((N_A_LAYERS, HEAD_DIM), 0.02),
        "a_k_g": 1.0 + nrm((N_A_LAYERS, HEAD_DIM), 0.02),
        "b_w_qkv": nrm((N_B_LAYERS, d, 3 * B_HEADS * HEAD_DIM), d ** -0.5),
        "b_w_o": nrm((N_B_LAYERS, B_HEADS * HEAD_DIM, d), (B_HEADS * HEAD_DIM) ** -0.5),
        "b_rpb": nrm((N_B_LAYERS, B_HEADS, 2 * B_WIN_R - 1, 2 * B_WIN_C - 1), 0.1),
        "c_w_qkv": nrm((N_C_LAYERS, d, 3 * d), d ** -0.5),
        "c_w_o": nrm((N_C_LAYERS, d, d), d ** -0.5),
        "c_lam_q1": nrm((N_C_LAYERS, HEAD_DIM), 0.1),
        "c_lam_k1": nrm((N_C_LAYERS, HEAD_DIM), 0.1),
        "c_lam_q2": nrm((N_C_LAYERS, HEAD_DIM), 0.1),
        "c_lam_k2": nrm((N_C_LAYERS, HEAD_DIM), 0.1),
        "c_subln_g": 1.0 + nrm((N_C_LAYERS, 2 * HEAD_DIM), 0.02),
        "final_g": 1.0 + nrm((d,), 0.02),
    }


def reference(x, c, ctx, c_ctx, norm1_g, norm2_g, mod_down, mod_up, mod_b, mlp_w1, mlp_w2,
              a_w_qkv, a_w_o, a_q_g, a_k_g, b_w_qkv, b_w_o, b_rpb,
              c_w_qkv, c_w_o, c_lam_q1, c_lam_k1, c_lam_q2, c_lam_k2, c_subln_g, final_g):
    cos, sin = axial_rope_tables(x.shape[1], HEAD_DIM)
    silu_c = jax.nn.silu(c)
    silu_cc = jax.nn.silu(c_ctx)
    h_lat, h_ctx = x, ctx
    for i in range(DEPTH):
        with_ctx_out = i < DEPTH - 1
        mod_l = (silu_c @ mod_down[i]) @ mod_up[i] + mod_b[i]
        mod_c = (silu_cc @ mod_down[i]) @ mod_up[i] + mod_b[i]
        sh1, sc1, g1, sh2, sc2, g2 = jnp.split(mod_l, N_MOD, axis=-1)
        cmod = jnp.split(mod_c, N_MOD, axis=-1)
        u_lat = modulate(rms_norm(h_lat, norm1_g[i]), sh1[:, None], sc1[:, None])
        u_ctx = modulate(rms_norm(h_ctx, norm1_g[i]), cmod[0], cmod[1])
        kind, slot = i % N_MIXERS, i // N_MIXERS
        if kind == 0:
            out_l, out_c = mixer_gqa(u_lat, u_ctx, a_w_qkv[slot], a_w_o[slot], a_q_g[slot], a_k_g[slot],
                                     cos, sin, with_ctx_out)
        elif kind == 1:
            out_l, out_c = mixer_neighbourhood(u_lat, u_ctx, b_w_qkv[slot], b_w_o[slot], b_rpb[slot], with_ctx_out)
        else:
            lambda_init = 0.8 - 0.6 * math.exp(-0.3 * i)
            out_l, out_c = mixer_diff(u_lat, u_ctx, c_w_qkv[slot], c_w_o[slot], c_lam_q1[slot], c_lam_k1[slot],
                                      c_lam_q2[slot], c_lam_k2[slot], c_subln_g[slot], cos, sin,
                                      lambda_init, with_ctx_out)
        h_lat = h_lat + g1[:, None] * out_l
        v_lat = modulate(rms_norm(h_lat, norm2_g[i]), sh2[:, None], sc2[:, None])
        h_lat = h_lat + g2[:, None] * channel_mixer(v_lat, mlp_w1[i], mlp_w2[i])
        if with_ctx_out:
            h_ctx = h_ctx + cmod[2] * out_c
            v_ctx = modulate(rms_norm(h_ctx, norm2_g[i]), cmod[3], cmod[4])
            h_ctx = h_ctx + cmod[5] * channel_mixer(v_ctx, mlp_w1[i], mlp_w2[i])
    return rms_norm(h_lat, final_g)
```

```python
import functools
import math

import jax
import jax.numpy as jnp
from jax import lax
from jax.experimental import pallas as pl
from jax.experimental.pallas import tpu as pltpu

F32 = jnp.float32
BF16 = jnp.bfloat16

LANES = 128
V7X_VMEM_BYTES = 64 * 2**20
VMEM_RESERVE_BYTES = 6 * 2**20

HEAD_DIM = 128
GRID_W = 64
WIN_R = 8
WIN_C = 16
MOD_ROWS = 8
N_MOD = 6
ROPE_THETA = 10000.0
EPS = 1e-6
LOG2E = 1.4426950408889634
NEG = -1e30
KEY_CHUNK = 512


def _vmem_limit(block_bytes):
    return int(min(block_bytes + VMEM_RESERVE_BYTES, V7X_VMEM_BYTES - 2**20))


def _params(sem, block_bytes):
    return pltpu.CompilerParams(dimension_semantics=sem, vmem_limit_bytes=_vmem_limit(block_bytes))


def _row_tile(seq, n_ctx_rows, cap=1024):
    tm = cap
    while seq % tm or n_ctx_rows % tm:
        tm //= 2
    return tm


def _mod_row(tile, tm, seq, n_batch):
    start = tile * tm
    return jnp.where(start < n_batch * seq, start // seq, n_batch)


def _mod_down_kernel(c_ref, w_ref, o_ref):
    c = c_ref[...]
    s = c * jax.nn.sigmoid(c)
    o_ref[0] = jnp.dot(s.astype(BF16), w_ref[0].astype(BF16), preferred_element_type=F32)


def _mod_up_kernel(t_ref, w_ref, b_ref, o_ref):
    o_ref[0] = jnp.dot(t_ref[0].astype(BF16), w_ref[0].astype(BF16),
                       preferred_element_type=F32) + b_ref[0]


def _modulation(cond, mod_down, mod_up, mod_b):
    depth, d, rank = mod_down.shape
    n_out = mod_up.shape[2]
    tr = 512
    t = pl.pallas_call(
        _mod_down_kernel,
        out_shape=jax.ShapeDtypeStruct((depth, MOD_ROWS, rank), F32),
        grid=(depth, rank // tr),
        in_specs=[pl.BlockSpec((MOD_ROWS, d), lambda l, j: (0, 0)),
                  pl.BlockSpec((1, d, tr), lambda l, j: (l, 0, j))],
        out_specs=pl.BlockSpec((1, MOD_ROWS, tr), lambda l, j: (l, 0, j)),
        compiler_params=_params(("arbitrary", "arbitrary"), 2 * d * tr * 4 + d * tr * 2 + 4 * MOD_ROWS * d * 4),
        name="mod_down",
    )(cond, mod_down)
    tn = 2048
    return pl.pallas_call(
        _mod_up_kernel,
        out_shape=jax.ShapeDtypeStruct((depth, MOD_ROWS, n_out), F32),
        grid=(depth, n_out // tn),
        in_specs=[pl.BlockSpec((1, MOD_ROWS, rank), lambda l, j: (l, 0, 0)),
                  pl.BlockSpec((1, rank, tn), lambda l, j: (l, 0, j)),
                  pl.BlockSpec((1, 1, tn), lambda l, j: (l, 0, j))],
        out_specs=pl.BlockSpec((1, MOD_ROWS, tn), lambda l, j: (l, 0, j)),
        compiler_params=_params(("arbitrary", "arbitrary"), 2 * rank * tn * 4 + rank * tn * 2),
        name="mod_up",
    )(t, mod_up, mod_b.reshape(depth, 1, n_out))


def _norm_mod_kernel(h_ref, g_ref, sh_ref, sc_ref, o_ref, *, tm, seq, n_batch):
    row = _mod_row(pl.program_id(0), tm, seq, n_batch)
    x = h_ref[...]
    y = x * lax.rsqrt(jnp.mean(x * x, axis=-1, keepdims=True) + EPS)
    y = y * g_ref[...]
    o_ref[...] = (y * (1.0 + sc_ref[pl.ds(row, 1), :]) + sh_ref[pl.ds(row, 1), :]).astype(o_ref.dtype)


def _norm_mod(h, gain, mod, shift_chunk, scale_chunk, n_rows, seq, n_batch, out_dtype):
    d = h.shape[1]
    tm = 256
    kern = functools.partial(_norm_mod_kernel, tm=tm, seq=seq, n_batch=n_batch)
    return pl.pallas_call(
        kern,
        out_shape=jax.ShapeDtypeStruct((n_rows, d), out_dtype),
        grid=(n_rows // tm,),
        in_specs=[pl.BlockSpec((tm, d), lambda i: (i, 0)),
                  pl.BlockSpec((1, d), lambda i: (0, 0)),
                  pl.BlockSpec((MOD_ROWS, d), lambda i: (0, shift_chunk)),
                  pl.BlockSpec((MOD_ROWS, d), lambda i: (0, scale_chunk))],
        out_specs=pl.BlockSpec((tm, d), lambda i: (i, 0)),
        compiler_params=_params(("arbitrary",), 2 * tm * d * 4 + 2 * tm * d * 4 + 3 * tm * d * 4),
        name="norm_mod",
    )(h, gain.reshape(1, d), mod, mod)


def _mm_act_kernel(x_ref, w_ref, o_ref, *, relu2):
    acc = jnp.dot(x_ref[...], w_ref[...], preferred_element_type=F32)
    if relu2:
        acc = jnp.square(jnp.maximum(acc, 0.0))
    o_ref[...] = acc.astype(o_ref.dtype)


def _mm_act(x, w, n_rows, relu2, tm):
    k = x.shape[1]
    n = w.shape[1]
    tn = 1024
    return pl.pallas_call(
        functools.partial(_mm_act_kernel, relu2=relu2),
        out_shape=jax.ShapeDtypeStruct((n_rows, n), BF16),
        grid=(n_rows // tm, n // tn),
        in_specs=[pl.BlockSpec((tm, k), lambda i, j: (i, 0)),
                  pl.BlockSpec((k, tn), lambda i, j: (0, j))],
        out_specs=pl.BlockSpec((tm, tn), lambda i, j: (i, j)),
        compiler_params=_params(("arbitrary", "arbitrary"),
                                2 * tm * k * 2 + 2 * k * tn * 2 + 2 * tm * tn * 2 + 2 * tm * tn * 4),
        name="mm_relu2" if relu2 else "mm_plain",
    )(x, w)


def _mm_resid_kernel(x_ref, w_ref, h_ref, g_ref, o_ref, acc_ref, *, tm, seq, n_batch, nk):
    kk = pl.program_id(2)

    @pl.when(kk == 0)
    def _():
        acc_ref[...] = jnp.zeros_like(acc_ref)

    acc_ref[...] += jnp.dot(x_ref[...], w_ref[...], preferred_element_type=F32)

    @pl.when(kk == nk - 1)
    def _():
        row = _mod_row(pl.program_id(0), tm, seq, n_batch)
        o_ref[...] = h_ref[...] + g_ref[pl.ds(row, 1), :] * acc_ref[...]


def _mm_resid(x, w, h, mod, gate_chunk, n_rows, seq, n_batch, tm):
    k = x.shape[1]
    n = w.shape[1]
    tn, tk = 1024, min(k, 2048)
    nk = k // tk
    gate_blk = gate_chunk * (n // tn)
    kern = functools.partial(_mm_resid_kernel, tm=tm, seq=seq, n_batch=n_batch, nk=nk)
    return pl.pallas_call(
        kern,
        out_shape=jax.ShapeDtypeStruct((n_rows, n), F32),
        grid=(n_rows // tm, n // tn, nk),
        in_specs=[pl.BlockSpec((tm, tk), lambda i, j, kk: (i, kk)),
                  pl.BlockSpec((tk, tn), lambda i, j, kk: (kk, j)),
                  pl.BlockSpec((tm, tn), lambda i, j, kk: (i, j)),
                  pl.BlockSpec((MOD_ROWS, tn), lambda i, j, kk: (0, gate_blk + j))],
        out_specs=pl.BlockSpec((tm, tn), lambda i, j, kk: (i, j)),
        scratch_shapes=[pltpu.VMEM((tm, tn), F32)],
        compiler_params=_params(("arbitrary", "arbitrary", "arbitrary"),
                                2 * tm * tk * 2 + 2 * tk * tn * 2 + 4 * tm * tn * 4 + 2 * tm * tn * 4),
        name="mm_resid",
    )(x, w, h, mod)


def _qkv_kernel(x_ref, w_ref, cos_ref, sin_ref, gq_ref, gk_ref, o_ref, *, kind, nq, nk, qscale):
    j = pl.program_id(1)
    acc = jnp.dot(x_ref[...], w_ref[...], preferred_element_type=F32)
    n_heads = acc.shape[1] // HEAD_DIM

    def per_head(fn):
        for hh in range(n_heads):
            sl = slice(hh * HEAD_DIM, (hh + 1) * HEAD_DIM)
            o_ref[:, sl] = fn(acc[:, sl]).astype(o_ref.dtype)

    def norm(x, g_ref):
        return x * lax.rsqrt(jnp.mean(x * x, axis=-1, keepdims=True) + EPS) * g_ref[...]

    def rope(x):
        return x * cos_ref[...] + pltpu.roll(x, HEAD_DIM // 2, 1) * sin_ref[...]

    if kind == "A":
        q_fn = lambda x: rope(norm(x, gq_ref)) * qscale
        k_fn = lambda x: rope(norm(x, gk_ref))
    elif kind == "C":
        q_fn = lambda x: rope(x) * qscale
        k_fn = rope
    else:
        q_fn = lambda x: x * qscale
        k_fn = None

    @pl.when(j < nq)
    def _():
        per_head(q_fn)

    if k_fn is None:
        @pl.when(j >= nq)
        def _():
            o_ref[...] = acc.astype(o_ref.dtype)
    else:
        @pl.when((j >= nq) & (j < nq + nk))
        def _():
            per_head(k_fn)

        @pl.when(j >= nq + nk)
        def _():
            o_ref[...] = acc.astype(o_ref.dtype)


def _mm_qkv(x, w, cos, sin, gq, gk, kind, n_q_cols, n_k_cols, seq, n_batch, tm):
    n_rows, k = x.shape
    n = w.shape[1]
    tn = 1024
    n_lat_tiles = n_batch * seq // tm
    tiles_per_seq = seq // tm
    qscale = HEAD_DIM ** -0.5 * LOG2E

    def rope_map(i, j):
        return (jnp.where(i < n_lat_tiles, i % tiles_per_seq, tiles_per_seq), 0)

    kern = functools.partial(_qkv_kernel, kind=kind, nq=n_q_cols // tn, nk=n_k_cols // tn, qscale=qscale)
    return pl.pallas_call(
        kern,
        out_shape=jax.ShapeDtypeStruct((n_rows, n), BF16),
        grid=(n_rows // tm, n // tn),
        in_specs=[pl.BlockSpec((tm, k), lambda i, j: (i, 0)),
                  pl.BlockSpec((k, tn), lambda i, j: (0, j)),
                  pl.BlockSpec((tm, HEAD_DIM), rope_map),
                  pl.BlockSpec((tm, HEAD_DIM), rope_map),
                  pl.BlockSpec((1, HEAD_DIM), lambda i, j: (0, 0)),
                  pl.BlockSpec((1, HEAD_DIM), lambda i, j: (0, 0))],
        out_specs=pl.BlockSpec((tm, tn), lambda i, j: (i, j)),
        compiler_params=_params(("arbitrary", "arbitrary"),
                                2 * tm * k * 2 + 2 * k * tn * 2 + 2 * tm * tn * 2 + 2 * tm * tn * 4
                                + 4 * tm * HEAD_DIM * 4),
        name="mm_qkv_" + kind,
    )(x, w, cos, sin, gq, gk)


def _softmax_step(state, k_blk, v_blk, q):
    m, l, acc = state
    s = lax.dot_general(k_blk, q, (((1,), (1,)), ((), ())), preferred_element_type=F32)
    m_new = jnp.maximum(m, jnp.max(s, axis=0, keepdims=True))
    alpha = jnp.exp2(m - m_new)
    p = jnp.exp2(s - m_new)
    l = alpha * l + jnp.sum(p, axis=0, keepdims=True)
    pv = lax.dot_general(v_blk, p.astype(BF16), (((0,), (0,)), ((), ())), preferred_element_type=F32)
    return m_new, l, alpha * acc + pv


def _softmax_init(tq, dv):
    return (jnp.full((1, tq), NEG, F32), jnp.zeros((1, tq), F32), jnp.zeros((dv, tq), F32))


def _attend(q, kv_refs, dk_slice):
    tq = q.shape[0]
    dv = kv_refs[0][1].shape[1]
    state = _softmax_init(tq, dv)
    for k_ref, v_ref in kv_refs:
        n_keys = k_ref.shape[0]
        step = min(KEY_CHUNK, n_keys)
        for c in range(n_keys // step):
            rows = slice(c * step, (c + 1) * step)
            state = _softmax_step(state, k_ref[rows, dk_slice], v_ref[rows, :], q)
    m, l, acc = state
    return acc / l


def _gqa_kernel(q_ref, *refs):
    o_ref = refs[-1]
    kv = [(refs[i], refs[i + 1]) for i in range(0, len(refs) - 1, 2)]
    o_t = _attend(q_ref[...], kv, slice(None))
    o_ref[...] = o_t.T.astype(o_ref.dtype)


def _gqa_attention(y, n_batch, seq, ctx_len, n_heads, group, k_col0, v_col0, latent):
    n_lat = n_batch * seq
    d_out = n_heads * HEAD_DIM
    if latent:
        tq = 512
        nq = seq // tq
        q_map = lambda b, h, i: (b * nq + i, h)
        n_out = n_lat
    else:
        tq = ctx_len
        nq = 1
        q_map = lambda b, h, i: (n_lat // ctx_len + b, h)
        n_out = n_batch * ctx_len
    ctx_row0 = n_lat // ctx_len
    in_specs = [pl.BlockSpec((tq, HEAD_DIM), q_map)]
    args = [y]
    blk = 2 * tq * HEAD_DIM * 2 * 2
    if latent:
        in_specs += [pl.BlockSpec((seq, HEAD_DIM), lambda b, h, i: (b, k_col0 + h // group)),
                     pl.BlockSpec((seq, HEAD_DIM), lambda b, h, i: (b, v_col0 + h // group))]
        args += [y, y]
        blk += 4 * seq * HEAD_DIM * 2
    in_specs += [pl.BlockSpec((ctx_len, HEAD_DIM), lambda b, h, i: (ctx_row0 + b, k_col0 + h // group)),
                 pl.BlockSpec((ctx_len, HEAD_DIM), lambda b, h, i: (ctx_row0 + b, v_col0 + h // group))]
    args += [y, y]
    if latent:
        out_map = lambda b, h, i: (b * nq + i, h)
    else:
        out_map = lambda b, h, i: (b, h)
    return pl.pallas_call(
        _gqa_kernel,
        out_shape=jax.ShapeDtypeStruct((n_out, d_out), BF16),
        grid=(n_batch, n_heads, nq),
        in_specs=in_specs,
        out_specs=pl.BlockSpec((tq, HEAD_DIM), out_map),
        compiler_params=_params(("arbitrary", "arbitrary", "arbitrary"), blk + 8 * KEY_CHUNK * tq * 4),
        name="attn_lat" if latent else "attn_ctx",
    )(*args)


def _diff_kernel(lam_ref, g_ref, q_ref, *refs, lambda_init):
    o_ref = refs[-1]
    kv = [(refs[i], refs[i + 1]) for i in range(0, len(refs) - 1, 2)]
    lq1, lk1, lq2, lk2 = (lam_ref[r:r + 1, :] for r in range(4))
    lam = (jnp.exp(jnp.sum(lq1 * lk1, axis=-1, keepdims=True))
           - jnp.exp(jnp.sum(lq2 * lk2, axis=-1, keepdims=True)) + lambda_init)
    q = q_ref[...]
    o0 = _attend(q[:, :HEAD_DIM], kv, slice(0, HEAD_DIM))
    o1 = _attend(q[:, HEAD_DIM:], kv, slice(HEAD_DIM, 2 * HEAD_DIM))
    o = (o0 - lam * o1).T
    o = o * lax.rsqrt(jnp.mean(o * o, axis=-1, keepdims=True) + EPS) * g_ref[...]
    o_ref[...] = (o * (1.0 - lambda_init)).astype(o_ref.dtype)


def _diff_attention(y, lam_vecs, subln_g, lambda_init, n_batch, seq, ctx_len, n_heads, latent):
    dh = 2 * HEAD_DIM
    n_lat = n_batch * seq
    ctx_row0 = n_lat // ctx_len
    if latent:
        tq = 512
        nq = seq // tq
        q_map = lambda b, h, i: (b * nq + i, h)
        out_map = q_map
        n_out = n_lat
    else:
        tq = ctx_len
        nq = 1
        q_map = lambda b, h, i: (ctx_row0 + b, h)
        out_map = lambda b, h, i: (b, h)
        n_out = n_batch * ctx_len
    in_specs = [pl.BlockSpec((8, HEAD_DIM), lambda b, h, i: (0, 0)),
                pl.BlockSpec((1, dh), lambda b, h, i: (0, 0)),
                pl.BlockSpec((tq, dh), q_map)]
    args = [lam_vecs, subln_g.reshape(1, dh), y]
    blk = 2 * tq * dh * 2 * 2
    if latent:
        in_specs += [pl.BlockSpec((seq, dh), lambda b, h, i: (b, n_heads + h)),
                     pl.BlockSpec((seq, dh), lambda b, h, i: (b, 2 * n_heads + h))]
        args += [y, y]
        blk += 4 * seq * dh * 2
    in_specs += [pl.BlockSpec((ctx_len, dh), lambda b, h, i: (ctx_row0 + b, n_heads + h)),
                 pl.BlockSpec((ctx_len, dh), lambda b, h, i: (ctx_row0 + b, 2 * n_heads + h))]
    args += [y, y]
    return pl.pallas_call(
        functools.partial(_diff_kernel, lambda_init=lambda_init),
        out_shape=jax.ShapeDtypeStruct((n_out, n_heads * dh), BF16),
        grid=(n_batch, n_heads, nq),
        in_specs=in_specs,
        out_specs=pl.BlockSpec((tq, dh), out_map),
        compiler_params=_params(("arbitrary", "arbitrary", "arbitrary"), blk + 10 * KEY_CHUNK * tq * 4),
        name="diff_lat" if latent else "diff_ctx",
    )(*args)


NB_Q_ROWS = 8
NB_K_ROWS = 16


def _nb_chunks(rows):
    wr = min(WIN_R, rows)
    chunks, patterns = [], []
    for c in range(rows // NB_Q_ROWS):
        kr0 = min(max(NB_Q_ROWS * c - WIN_R // 2, 0), rows - NB_K_ROWS)
        pat = []
        for a in range(NB_K_ROWS):
            for rq in range(NB_Q_ROWS):
                r = NB_Q_ROWS * c + rq
                r0 = min(max(r - WIN_R // 2, 0), rows - wr)
                kr = kr0 + a
                pat.append(kr - r + (WIN_R - 1) if r0 <= kr < r0 + wr else None)
        pat = tuple(pat)
        if pat not in patterns:
            patterns.append(pat)
        chunks.append((kr0, patterns.index(pat)))
    pairs = []
    for pat in patterns:
        for t in range(0, len(pat), 2):
            if pat[t:t + 2] not in pairs:
                pairs.append(pat[t:t + 2])
    return chunks, patterns, pairs


def _nb_kernel(tz_ref, q_ref, k_ref, v_ref, kc_ref, vc_ref, o_ref, bias_ref, *, chunks, patterns, pairs):
    qw = NB_Q_ROWS * GRID_W
    kw = NB_K_ROWS * GRID_W

    @pl.when(pl.program_id(1) == 0)
    def _():
        for p, pat in enumerate(patterns):
            for a in range(NB_K_ROWS):
                for j in range(NB_Q_ROWS // 2):
                    t = a * NB_Q_ROWS + 2 * j
                    bias_ref[p, a * GRID_W:(a + 1) * GRID_W, j * LANES:(j + 1) * LANES] = (
                        tz_ref[0, pairs.index(pat[t:t + 2])])

    kctx = kc_ref[...]
    vctx = vc_ref[...]
    for c, (kr0, p) in enumerate(chunks):
        q = q_ref[c * qw:(c + 1) * qw, :]
        kb = k_ref[kr0 * GRID_W:kr0 * GRID_W + kw, :]
        vb = v_ref[kr0 * GRID_W:kr0 * GRID_W + kw, :]
        dims = (((1,), (1,)), ((), ()))
        s_w = lax.dot_general(kb, q, dims, preferred_element_type=F32) + bias_ref[p]
        s_c = lax.dot_general(kctx, q, dims, preferred_element_type=F32)
        m = jnp.maximum(jnp.max(s_w, axis=0, keepdims=True), jnp.max(s_c, axis=0, keepdims=True))
        p_w = jnp.exp2(s_w - m)
        p_c = jnp.exp2(s_c - m)
        l = jnp.sum(p_w, axis=0, keepdims=True) + jnp.sum(p_c, axis=0, keepdims=True)
        tdims = (((0,), (0,)), ((), ()))
        o_t = (lax.dot_general(vb, p_w.astype(BF16), tdims, preferred_element_type=F32)
               + lax.dot_general(vctx, p_c.astype(BF16), tdims, preferred_element_type=F32))
        o_ref[c * qw:(c + 1) * qw, :] = (o_t / l).T.astype(o_ref.dtype)


def _nb_bias_tiles(rpb, pairs):
    kc = jnp.arange(GRID_W, dtype=jnp.int32)[:, None]
    qc = jnp.arange(GRID_W, dtype=jnp.int32)[None, :]
    c0 = jnp.clip(qc - WIN_C // 2, 0, GRID_W - WIN_C)
    in_win = (kc >= c0) & (kc < c0 + WIN_C)
    dc = jnp.clip(kc - qc + (WIN_C - 1), 0, 2 * WIN_C - 2)
    tiles = jnp.where(in_win[None, None], rpb[:, :, dc] * LOG2E, NEG)
    neg = jnp.full((rpb.shape[0], GRID_W, GRID_W), NEG, F32)
    half = lambda dr: neg if dr is None else tiles[:, dr]
    return jnp.stack([jnp.concatenate([half(e), half(o)], axis=-1) for e, o in pairs], axis=1)


def _nb_attention(y, rpb, n_batch, seq, ctx_len, n_heads):
    rows = seq // GRID_W
    assert seq % GRID_W == 0 and rows % NB_Q_ROWS == 0 and rows >= NB_K_ROWS
    chunks, patterns, pairs = _nb_chunks(rows)
    tz = _nb_bias_tiles(rpb, pairs)
    n_lat = n_batch * seq
    ctx_row0 = n_lat // ctx_len
    qw = NB_Q_ROWS * GRID_W
    kw = NB_K_ROWS * GRID_W
    kern = functools.partial(_nb_kernel, chunks=tuple(chunks), patterns=tuple(patterns), pairs=tuple(pairs))
    return pl.pallas_call(
        kern,
        out_shape=jax.ShapeDtypeStruct((n_lat, n_heads * HEAD_DIM), BF16),
        grid=(n_heads, n_batch),
        in_specs=[pl.BlockSpec((1, len(pairs), GRID_W, 2 * GRID_W), lambda h, b: (h, 0, 0, 0)),
                  pl.BlockSpec((seq, HEAD_DIM), lambda h, b: (b, h)),
                  pl.BlockSpec((seq, HEAD_DIM), lambda h, b: (b, n_heads + h)),
                  pl.BlockSpec((seq, HEAD_DIM), lambda h, b: (b, 2 * n_heads + h)),
                  pl.BlockSpec((ctx_len, HEAD_DIM), lambda h, b: (ctx_row0 + b, n_heads + h)),
                  pl.BlockSpec((ctx_len, HEAD_DIM), lambda h, b: (ctx_row0 + b, 2 * n_heads + h))],
        out_specs=pl.BlockSpec((seq, HEAD_DIM), lambda h, b: (b, h)),
        scratch_shapes=[pltpu.VMEM((len(patterns), kw, qw), F32)],
        compiler_params=_params(("arbitrary", "arbitrary"),
                                8 * seq * HEAD_DIM * 2 + len(patterns) * kw * qw * 4
                                + 2 * len(pairs) * GRID_W * LANES * 4 + 6 * (kw + ctx_len) * qw * 4),
        name="attn_nb",
    )(tz, y, y, y, y, y)


def _rope_tables(seq, pad_rows):
    t = jnp.arange(seq, dtype=jnp.int32)
    row = (t // GRID_W).astype(F32)
    col = (t % GRID_W).astype(F32)
    n_freq = HEAD_DIM // 4
    inv_freq = ROPE_THETA ** (-jnp.arange(n_freq, dtype=F32) / n_freq)
    ang = jnp.concatenate([row[:, None] * inv_freq, col[:, None] * inv_freq], axis=-1)
    cos, sin = jnp.cos(ang), jnp.sin(ang)
    cos2 = jnp.concatenate([cos, cos], axis=-1)
    sin2 = jnp.concatenate([-sin, sin], axis=-1)
    cos2 = jnp.concatenate([cos2, jnp.ones((pad_rows, HEAD_DIM), F32)], axis=0)
    sin2 = jnp.concatenate([sin2, jnp.zeros((pad_rows, HEAD_DIM), F32)], axis=0)
    return cos2, sin2


def _split_pairs(w, n_cols):
    k = w.shape[0]
    a = w[:, :n_cols].reshape(k, n_cols // HEAD_DIM, HEAD_DIM // 2, 2)
    a = jnp.swapaxes(a, -1, -2).reshape(k, n_cols)
    return jnp.concatenate([a, w[:, n_cols:]], axis=-1)


def kernel(x, c, ctx, c_ctx, norm1_g, norm2_g, mod_down, mod_up, mod_b, mlp_w1, mlp_w2, a_w_qkv, a_w_o, a_q_g, a_k_g, b_w_qkv, b_w_o, b_rpb, c_w_qkv, c_w_o, c_lam_q1, c_lam_k1, c_lam_q2, c_lam_k2, c_subln_g, final_g):
    n_batch, seq, d = x.shape
    ctx_len = ctx.shape[1]
    depth = norm1_g.shape[0]
    n_lat = n_batch * seq
    n_all = n_lat + n_batch * ctx_len
    n_mixers = 3
    heads = d // HEAD_DIM

    h = jnp.concatenate([x.reshape(n_lat, d), ctx.reshape(n_batch * ctx_len, d)], axis=0)
    cond = jnp.concatenate([c, c_ctx[None], jnp.zeros((MOD_ROWS - n_batch - 1, d), F32)], axis=0)
    mods = _modulation(cond, mod_down, mod_up, mod_b)
    tm = _row_tile(seq, n_batch * ctx_len)
    cos, sin = _rope_tables(seq, tm)
    ones_g = jnp.ones((1, HEAD_DIM), F32)

    for i in range(depth):
        with_ctx_out = i < depth - 1
        n_rows = n_all if with_ctx_out else n_lat
        mod = mods[i]
        kind, slot = i % n_mixers, i // n_mixers
        u = _norm_mod(h, norm1_g[i], mod, 0, 1, n_all, seq, n_batch, BF16)
        if kind == 0:
            kv_heads = heads // 4
            n_q, n_kv = heads * HEAD_DIM, kv_heads * HEAD_DIM
            w = _split_pairs(a_w_qkv[slot], n_q + n_kv).astype(BF16)
            gq = _split_pairs(a_q_g[slot][None], HEAD_DIM)
            gk = _split_pairs(a_k_g[slot][None], HEAD_DIM)
            y = _mm_qkv(u, w, cos, sin, gq, gk, "A", n_q, n_kv, seq, n_batch, tm)
            o = _gqa_attention(y, n_batch, seq, ctx_len, heads, 4, heads, heads + kv_heads, True)
            if with_ctx_out:
                o_c = _gqa_attention(y, n_batch, seq, ctx_len, heads, 4, heads, heads + kv_heads, False)
            w_o = a_w_o[slot]
        elif kind == 1:
            y = _mm_qkv(u, b_w_qkv[slot].astype(BF16), cos, sin, ones_g, ones_g, "B", d, d, seq, n_batch, tm)
            o = _nb_attention(y, b_rpb[slot], n_batch, seq, ctx_len, heads)
            if with_ctx_out:
                o_c = _gqa_attention(y, n_batch, seq, ctx_len, heads, 1, heads, 2 * heads, False)
            w_o = b_w_o[slot]
        else:
            lambda_init = 0.8 - 0.6 * math.exp(-0.3 * i)
            w = _split_pairs(c_w_qkv[slot], 2 * d).astype(BF16)
            y = _mm_qkv(u, w, cos, sin, ones_g, ones_g, "C", d, d, seq, n_batch, tm)
            lam_vecs = jnp.concatenate([c_lam_q1[slot][None], c_lam_k1[slot][None], c_lam_q2[slot][None],
                                        c_lam_k2[slot][None], jnp.zeros((4, HEAD_DIM), F32)], axis=0)
            o = _diff_attention(y, lam_vecs, c_subln_g[slot], lambda_init, n_batch, seq, ctx_len, heads // 2, True)
            if with_ctx_out:
                o_c = _diff_attention(y, lam_vecs, c_subln_g[slot], lambda_init, n_batch, seq, ctx_len,
                                      heads // 2, False)
            w_o = c_w_o[slot]
        if with_ctx_out:
            o = jnp.concatenate([o, o_c], axis=0)
        h = _mm_resid(o, w_o.astype(BF16), h, mod, 2, n_rows, seq, n_batch, tm)
        v = _norm_mod(h, norm2_g[i], mod, 3, 4, n_rows, seq, n_batch, BF16)
        a = _mm_act(v, mlp_w1[i].astype(BF16), n_rows, True, tm)
        h = _mm_resid(a, mlp_w2[i].astype(BF16), h, mod, 5, n_rows, seq, n_batch, tm)

    zeros_mod = jnp.zeros((MOD_ROWS, N_MOD * d), F32)
    out = _norm_mod(h, final_g, zeros_mod, 0, 1, n_lat, seq, n_batch, F32)
    return out.reshape(n_batch, seq, d)
```

```python
import functools
import math

import jax
import jax.numpy as jnp
from jax import lax
from jax.experimental import pallas as pl
from jax.experimental.pallas import tpu as pltpu

F32 = jnp.float32
BF16 = jnp.bfloat16

LANES = 128
V7X_VMEM_BYTES = 64 * 2**20
VMEM_RESERVE_BYTES = 6 * 2**20

HEAD_DIM = 128
GRID_W = 64
WIN_R = 8
WIN_C = 16
MOD_ROWS = 8
N_MOD = 6
ROPE_THETA = 10000.0
EPS = 1e-6
LOG2E = 1.4426950408889634
NEG = -1e30
KEY_CHUNK = 512


def _vmem_limit(block_bytes):
    return int(min(block_bytes + VMEM_RESERVE_BYTES, V7X_VMEM_BYTES - 2**20))


def _params(sem, block_bytes):
    return pltpu.CompilerParams(dimension_semantics=sem, vmem_limit_bytes=_vmem_limit(block_bytes))


def _row_tile(seq, n_ctx_rows, cap=1024):
    tm = cap
    while seq % tm or n_ctx_rows % tm:
        tm //= 2
    return tm


def _mod_row(tile, tm, seq, n_batch):
    start = tile * tm
    return jnp.where(start < n_batch * seq, start // seq, n_batch)


def _mod_down_kernel(c_ref, w_ref, o_ref):
    c = c_ref[...]
    s = c * jax.nn.sigmoid(c)
    o_ref[0] = jnp.dot(s.astype(BF16), w_ref[0].astype(BF16), preferred_element_type=F32)


def _mod_up_kernel(t_ref, w_ref, b_ref, o_ref):
    o_ref[0] = jnp.dot(t_ref[0].astype(BF16), w_ref[0].astype(BF16),
                       preferred_element_type=F32) + b_ref[0]


def _modulation(cond, mod_down, mod_up, mod_b):
    depth, d, rank = mod_down.shape
    n_out = mod_up.shape[2]
    tr = 512
    t = pl.pallas_call(
        _mod_down_kernel,
        out_shape=jax.ShapeDtypeStruct((depth, MOD_ROWS, rank), F32),
        grid=(depth, rank // tr),
        in_specs=[pl.BlockSpec((MOD_ROWS, d), lambda l, j: (0, 0)),
                  pl.BlockSpec((1, d, tr), lambda l, j: (l, 0, j))],
        out_specs=pl.BlockSpec((1, MOD_ROWS, tr), lambda l, j: (l, 0, j)),
        compiler_params=_params(("arbitrary", "arbitrary"), 2 * d * tr * 4 + d * tr * 2 + 4 * MOD_ROWS * d * 4),
        name="mod_down",
    )(cond, mod_down)
    tn = 2048
    return pl.pallas_call(
        _mod_up_kernel,
        out_shape=jax.ShapeDtypeStruct((depth, MOD_ROWS, n_out), F32),
        grid=(depth, n_out // tn),
        in_specs=[pl.BlockSpec((1, MOD_ROWS, rank), lambda l, j: (l, 0, 0)),
                  pl.BlockSpec((1, rank, tn), lambda l, j: (l, 0, j)),
                  pl.BlockSpec((1, 1, tn), lambda l, j: (l, 0, j))],
        out_specs=pl.BlockSpec((1, MOD_ROWS, tn), lambda l, j: (l, 0, j)),
        compiler_params=_params(("arbitrary", "arbitrary"), 2 * rank * tn * 4 + rank * tn * 2),
        name="mod_up",
    )(t, mod_up, mod_b.reshape(depth, 1, n_out))


def _norm_mod_kernel(h_ref, g_ref, sh_ref, sc_ref, o_ref, *, tm, seq, n_batch):
    row = _mod_row(pl.program_id(0), tm, seq, n_batch)
    x = h_ref[...]
    y = x * lax.rsqrt(jnp.mean(x * x, axis=-1, keepdims=True) + EPS)
    y = y * g_ref[...]
    o_ref[...] = (y * (1.0 + sc_ref[pl.ds(row, 1), :]) + sh_ref[pl.ds(row, 1), :]).astype(o_ref.dtype)


def _norm_mod(h, gain, mod, shift_chunk, scale_chunk, n_rows, seq, n_batch, out_dtype):
    d = h.shape[1]
    tm = 256
    kern = functools.partial(_norm_mod_kernel, tm=tm, seq=seq, n_batch=n_batch)
    return pl.pallas_call(
        kern,
        out_shape=jax.ShapeDtypeStruct((n_rows, d), out_dtype),
        grid=(n_rows // tm,),
        in_specs=[pl.BlockSpec((tm, d), lambda i: (i, 0)),
                  pl.BlockSpec((1, d), lambda i: (0, 0)),
                  pl.BlockSpec((MOD_ROWS, d), lambda i: (0, shift_chunk)),
                  pl.BlockSpec((MOD_ROWS, d), lambda i: (0, scale_chunk))],
        out_specs=pl.BlockSpec((tm, d), lambda i: (i, 0)),
        compiler_params=_params(("arbitrary",), 2 * tm * d * 4 + 2 * tm * d * 4 + 3 * tm * d * 4),
        name="norm_mod",
    )(h, gain.reshape(1, d), mod, mod)


def _mm_act_kernel(x_ref, w_ref, o_ref, *, relu2):
    acc = jnp.dot(x_ref[...], w_ref[...], preferred_element_type=F32)
    if relu2:
        acc = jnp.square(jnp.maximum(acc, 0.0))
    o_ref[...] = acc.astype(o_ref.dtype)


def _mm_act(x, w, n_rows, relu2, tm):
    k = x.shape[1]
    n = w.shape[1]
    tn = 1024
    return pl.pallas_call(
        functools.partial(_mm_act_kernel, relu2=relu2),
        out_shape=jax.ShapeDtypeStruct((n_rows, n), BF16),
        grid=(n_rows // tm, n // tn),
        in_specs=[pl.BlockSpec((tm, k), lambda i, j: (i, 0)),
                  pl.BlockSpec((k, tn), lambda i, j: (0, j))],
        out_specs=pl.BlockSpec((tm, tn), lambda i, j: (i, j)),
        compiler_params=_params(("arbitrary", "arbitrary"),
                                2 * tm * k * 2 + 2 * k * tn * 2 + 2 * tm * tn * 2 + 2 * tm * tn * 4),
        name="mm_relu2" if relu2 else "mm_plain",
    )(x, w)


def _mm_resid_kernel(x_ref, w_ref, h_ref, g_ref, o_ref, acc_ref, *, tm, seq, n_batch, nk):
    kk = pl.program_id(2)

    @pl.when(kk == 0)
    def _():
        acc_ref[...] = jnp.zeros_like(acc_ref)

    acc_ref[...] += jnp.dot(x_ref[...], w_ref[...], preferred_element_type=F32)

    @pl.when(kk == nk - 1)
    def _():
        row = _mod_row(pl.program_id(0), tm, seq, n_batch)
        o_ref[...] = h_ref[...] + g_ref[pl.ds(row, 1), :] * acc_ref[...]


def _mm_resid(x, w, h, mod, gate_chunk, n_rows, seq, n_batch, tm):
    k = x.shape[1]
    n = w.shape[1]
    tn, tk = 1024, min(k, 2048)
    nk = k // tk
    gate_blk = gate_chunk * (n // tn)
    kern = functools.partial(_mm_resid_kernel, tm=tm, seq=seq, n_batch=n_batch, nk=nk)
    return pl.pallas_call(
        kern,
        out_shape=jax.ShapeDtypeStruct((n_rows, n), F32),
        grid=(n_rows // tm, n // tn, nk),
        in_specs=[pl.BlockSpec((tm, tk), lambda i, j, kk: (i, kk)),
                  pl.BlockSpec((tk, tn), lambda i, j, kk: (kk, j)),
                  pl.BlockSpec((tm, tn), lambda i, j, kk: (i, j)),
                  pl.BlockSpec((MOD_ROWS, tn), lambda i, j, kk: (0, gate_blk + j))],
        out_specs=pl.BlockSpec((tm, tn), lambda i, j, kk: (i, j)),
        scratch_shapes=[pltpu.VMEM((tm, tn), F32)],
        compiler_params=_params(("arbitrary", "arbitrary", "arbitrary"),
                                2 * tm * tk * 2 + 2 * tk * tn * 2 + 4 * tm * tn * 4 + 2 * tm * tn * 4),
        name="mm_resid",
    )(x, w, h, mod)


def _qkv_kernel(x_ref, w_ref, cos_ref, sin_ref, gq_ref, gk_ref, o_ref, *, kind, nq, nk, qscale):
    j = pl.program_id(1)
    acc = jnp.dot(x_ref[...], w_ref[...], preferred_element_type=F32)
    n_heads = acc.shape[1] // HEAD_DIM

    def per_head(fn):
        for hh in range(n_heads):
            sl = slice(hh * HEAD_DIM, (hh + 1) * HEAD_DIM)
            o_ref[:, sl] = fn(acc[:, sl]).astype(o_ref.dtype)

    def norm(x, g_ref):
        return x * lax.rsqrt(jnp.mean(x * x, axis=-1, keepdims=True) + EPS) * g_ref[...]

    def rope(x):
        return x * cos_ref[...] + pltpu.roll(x, HEAD_DIM // 2, 1) * sin_ref[...]

    if kind == "A":
        q_fn = lambda x: rope(norm(x, gq_ref)) * qscale
        k_fn = lambda x: rope(norm(x, gk_ref))
    elif kind == "C":
        q_fn = lambda x: rope(x) * qscale
        k_fn = rope
    else:
        q_fn = lambda x: x * qscale
        k_fn = None

    @pl.when(j < nq)
    def _():
        per_head(q_fn)

    if k_fn is None:
        @pl.when(j >= nq)
        def _():
            o_ref[...] = acc.astype(o_ref.dtype)
    else:
        @pl.when((j >= nq) & (j < nq + nk))
        def _():
            per_head(k_fn)

        @pl.when(j >= nq + nk)
        def _():
            o_ref[...] = acc.astype(o_ref.dtype)


def _mm_qkv(x, w, cos, sin, gq, gk, kind, n_q_cols, n_k_cols, seq, n_batch, tm):
    n_rows, k = x.shape
    n = w.shape[1]
    tn = 1024
    n_lat_tiles = n_batch * seq // tm
    tiles_per_seq = seq // tm
    qscale = HEAD_DIM ** -0.5 * LOG2E

    def rope_map(i, j):
        return (jnp.where(i < n_lat_tiles, i % tiles_per_seq, tiles_per_seq), 0)

    kern = functools.partial(_qkv_kernel, kind=kind, nq=n_q_cols // tn, nk=n_k_cols // tn, qscale=qscale)
    return pl.pallas_call(
        kern,
        out_shape=jax.ShapeDtypeStruct((n_rows, n), BF16),
        grid=(n_rows // tm, n // tn),
        in_specs=[pl.BlockSpec((tm, k), lambda i, j: (i, 0)),
                  pl.BlockSpec((k, tn), lambda i, j: (0, j)),
                  pl.BlockSpec((tm, HEAD_DIM), rope_map),
                  pl.BlockSpec((tm, HEAD_DIM), rope_map),
                  pl.BlockSpec((1, HEAD_DIM), lambda i, j: (0, 0)),
                  pl.BlockSpec((1, HEAD_DIM), lambda i, j: (0, 0))],
        out_specs=pl.BlockSpec((tm, tn), lambda i, j: (i, j)),
        compiler_params=_params(("arbitrary", "arbitrary"),
                                2 * tm * k * 2 + 2 * k * tn * 2 + 2 * tm * tn * 2 + 2 * tm * tn * 4
                                + 4 * tm * HEAD_DIM * 4),
        name="mm_qkv_" + kind,
    )(x, w, cos, sin, gq, gk)


def _softmax_step(state, k_blk, v_blk, q):
    m, l, acc = state
    s = lax.dot_general(k_blk, q, (((1,), (1,)), ((), ())), preferred_element_type=F32)
    m_new = jnp.maximum(m, jnp.max(s, axis=0, keepdims=True))
    alpha = jnp.exp2(m - m_new)
    p = jnp.exp2(s - m_new)
    l = alpha * l + jnp.sum(p, axis=0, keepdims=True)
    pv = lax.dot_general(v_blk, p.astype(BF16), (((0,), (0,)), ((), ())), preferred_element_type=F32)
    return m_new, l, alpha * acc + pv


def _softmax_init(tq, dv):
    return (jnp.full((1, tq), NEG, F32), jnp.zeros((1, tq), F32), jnp.zeros((dv, tq), F32))


def _attend(q, kv_refs, dk_slice):
    tq = q.shape[0]
    dv = kv_refs[0][1].shape[1]
    state = _softmax_init(tq, dv)
    for k_ref, v_ref in kv_refs:
        n_keys = k_ref.shape[0]
        step = min(KEY_CHUNK, n_keys)
        for c in range(n_keys // step):
            rows = slice(c * step, (c + 1) * step)
            state = _softmax_step(state, k_ref[rows, dk_slice], v_ref[rows, :], q)
    m, l, acc = state
    return acc / l


def _gqa_kernel(q_ref, *refs):
    o_ref = refs[-1]
    kv = [(refs[i], refs[i + 1]) for i in range(0, len(refs) - 1, 2)]
    o_t = _attend(q_ref[...], kv, slice(None))
    o_ref[...] = o_t.T.astype(o_ref.dtype)


def _gqa_attention(y, n_batch, seq, ctx_len, n_heads, group, k_col0, v_col0, latent):
    n_lat = n_batch * seq
    d_out = n_heads * HEAD_DIM
    if latent:
        tq = 512
        nq = seq // tq
        q_map = lambda b, h, i: (b * nq + i, h)
        n_out = n_lat
    else:
        tq = ctx_len
        nq = 1
        q_map = lambda b, h, i: (n_lat // ctx_len + b, h)
        n_out = n_batch * ctx_len
    ctx_row0 = n_lat // ctx_len
    in_specs = [pl.BlockSpec((tq, HEAD_DIM), q_map)]
    args = [y]
    blk = 2 * tq * HEAD_DIM * 2 * 2
    if latent:
        in_specs += [pl.BlockSpec((seq, HEAD_DIM), lambda b, h, i: (b, k_col0 + h // group)),
                     pl.BlockSpec((seq, HEAD_DIM), lambda b, h, i: (b, v_col0 + h // group))]
        args += [y, y]
        blk += 4 * seq * HEAD_DIM * 2
    in_specs += [pl.BlockSpec((ctx_len, HEAD_DIM), lambda b, h, i: (ctx_row0 + b, k_col0 + h // group)),
                 pl.BlockSpec((ctx_len, HEAD_DIM), lambda b, h, i: (ctx_row0 + b, v_col0 + h // group))]
    args += [y, y]
    if latent:
        out_map = lambda b, h, i: (b * nq + i, h)
    else:
        out_map = lambda b, h, i: (b, h)
    return pl.pallas_call(
        _gqa_kernel,
        out_shape=jax.ShapeDtypeStruct((n_out, d_out), BF16),
        grid=(n_batch, n_heads, nq),
        in_specs=in_specs,
        out_specs=pl.BlockSpec((tq, HEAD_DIM), out_map),
        compiler_params=_params(("arbitrary", "arbitrary", "arbitrary"), blk + 8 * KEY_CHUNK * tq * 4),
        name="attn_lat" if latent else "attn_ctx",
    )(*args)


def _diff_kernel(lam_ref, g_ref, q_ref, *refs, lambda_init):
    o_ref = refs[-1]
    kv = [(refs[i], refs[i + 1]) for i in range(0, len(refs) - 1, 2)]
    lq1, lk1, lq2, lk2 = (lam_ref[r:r + 1, :] for r in range(4))
    lam = (jnp.exp(jnp.sum(lq1 * lk1, axis=-1, keepdims=True))
           - jnp.exp(jnp.sum(lq2 * lk2, axis=-1, keepdims=True)) + lambda_init)
    q = q_ref[...]
    o0 = _attend(q[:, :HEAD_DIM], kv, slice(0, HEAD_DIM))
    o1 = _attend(q[:, HEAD_DIM:], kv, slice(HEAD_DIM, 2 * HEAD_DIM))
    o = (o0 - lam * o1).T
    o = o * lax.rsqrt(jnp.mean(o * o, axis=-1, keepdims=True) + EPS) * g_ref[...]
    o_ref[...] = (o * (1.0 - lambda_init)).astype(o_ref.dtype)


def _diff_attention(y, lam_vecs, subln_g, lambda_init, n_batch, seq, ctx_len, n_heads, latent):
    dh = 2 * HEAD_DIM
    n_lat = n_batch * seq
    ctx_row0 = n_lat // ctx_len
    if latent:
        tq = 512
        nq = seq // tq
        q_map = lambda b, h, i: (b * nq + i, h)
        out_map = q_map
        n_out = n_lat
    else:
        tq = ctx_len
        nq = 1
        q_map = lambda b, h, i: (ctx_row0 + b, h)
        out_map = lambda b, h, i: (b, h)
        n_out = n_batch * ctx_len
    in_specs = [pl.BlockSpec((8, HEAD_DIM), lambda b, h, i: (0, 0)),
                pl.BlockSpec((1, dh), lambda b, h, i: (0, 0)),
                pl.BlockSpec((tq, dh), q_map)]
    args = [lam_vecs, subln_g.reshape(1, dh), y]
    blk = 2 * tq * dh * 2 * 2
    if latent:
        in_specs += [pl.BlockSpec((seq, dh), lambda b, h, i: (b, n_heads + h)),
                     pl.BlockSpec((seq, dh), lambda b, h, i: (b, 2 * n_heads + h))]
        args += [y, y]
        blk += 4 * seq * dh * 2
    in_specs += [pl.BlockSpec((ctx_len, dh), lambda b, h, i: (ctx_row0 + b, n_heads + h)),
                 pl.BlockSpec((ctx_len, dh), lambda b, h, i: (ctx_row0 + b, 2 * n_heads + h))]
    args += [y, y]
    return pl.pallas_call(
        functools.partial(_diff_kernel, lambda_init=lambda_init),
        out_shape=jax.ShapeDtypeStruct((n_out, n_heads * dh), BF16),
        grid=(n_batch, n_heads, nq),
        in_specs=in_specs,
        out_specs=pl.BlockSpec((tq, dh), out_map),
        compiler_params=_params(("arbitrary", "arbitrary", "arbitrary"), blk + 10 * KEY_CHUNK * tq * 4),
        name="diff_lat" if latent else "diff_ctx",
    )(*args)


PIPE_TQ = 512
PIPE_CHUNK = 256


def _pipe_kernel(*refs, diff, lambda_init):
    if diff:
        lam_ref, g_ref, q_ref, k_ref, v_ref, kc_ref, vc_ref, o_ref, sa_ref, sb_ref, m_ref, o0_ref = refs
    else:
        q_ref, k_ref, v_ref, kc_ref, vc_ref, o_ref, sa_ref, sb_ref, m_ref = refs
    j = pl.program_id(2)
    tq = q_ref.shape[0]
    dv = v_ref.shape[1]

    @pl.when(j == 0)
    def _():
        sb_ref[...] = jnp.zeros(sb_ref.shape, F32)
        m_ref[1] = jnp.zeros(m_ref.shape[1:], F32)
        if diff:
            o0_ref[...] = jnp.zeros(o0_ref.shape, F32)

    def body(s_cur, s_prev, cur, prev):
        q = q_ref[...]
        m_new = jnp.full((1, tq), NEG, F32)
        m_old = m_ref[prev][0:1, :]
        l = jnp.zeros((1, tq), F32)
        acc = jnp.zeros((dv, tq), F32)
        blocks = [(k_ref, v_ref, c * PIPE_CHUNK, PIPE_CHUNK) for c in range(k_ref.shape[0] // PIPE_CHUNK)]
        blocks.append((kc_ref, vc_ref, 0, kc_ref.shape[0]))
        off = 0
        for kr, vr, r0, n in blocks:
            s = lax.dot_general(kr[r0:r0 + n, :], q, (((1,), (1,)), ((), ())), preferred_element_type=F32)
            s_cur[off:off + n, :] = s
            m_new = jnp.maximum(m_new, jnp.max(s, axis=0, keepdims=True))
            p = jnp.exp2(s_prev[off:off + n, :] - m_old)
            l = l + jnp.sum(p, axis=0, keepdims=True)
            acc = acc + lax.dot_general(vr[r0:r0 + n, :], p.astype(BF16), (((0,), (0,)), ((), ())),
                                        preferred_element_type=F32)
            off += n
        m_ref[cur] = jnp.broadcast_to(m_new, m_ref.shape[1:])
        o_t = acc / l
        if not diff:
            o_ref[...] = o_t.T.astype(o_ref.dtype)
        elif cur == 1:
            o0_ref[...] = o_t
        else:
            lq1, lk1, lq2, lk2 = (lam_ref[r:r + 1, :] for r in range(4))
            lam = (jnp.exp(jnp.sum(lq1 * lk1, axis=-1, keepdims=True))
                   - jnp.exp(jnp.sum(lq2 * lk2, axis=-1, keepdims=True)) + lambda_init)
            o = (o0_ref[...] - lam * o_t).T
            o = o * lax.rsqrt(jnp.mean(o * o, axis=-1, keepdims=True) + EPS) * g_ref[...]
            o_ref[...] = (o * (1.0 - lambda_init)).astype(o_ref.dtype)

    @pl.when(j % 2 == 0)
    def _():
        body(sa_ref, sb_ref, 0, 1)

    @pl.when(j % 2 == 1)
    def _():
        body(sb_ref, sa_ref, 1, 0)


def _pipe_attention(y, d_out, n_batch, seq, ctx_len, n_outer, items_per_qblock, heads_per_outer, dv,
                    q_col, k_col, v_col, out_col, extra_args, extra_specs, lambda_init, name):
    diff = lambda_init is not None
    n_lat = n_batch * seq
    ctx_row0 = n_lat // ctx_len
    tq = PIPE_TQ
    nq = seq // tq
    n_items = nq * items_per_qblock * heads_per_outer
    n_keys = seq + ctx_len

    def qblock(it):
        return (it // items_per_qblock) % nq

    def cur_item(j):
        return jnp.minimum(j, n_items - 1)

    def prev_item(j):
        return jnp.maximum(j - 1, 0)

    in_specs = list(extra_specs) + [
        pl.BlockSpec((tq, HEAD_DIM), lambda b, g, j: (b * nq + qblock(cur_item(j)), q_col(g, cur_item(j)))),
        pl.BlockSpec((seq, HEAD_DIM), lambda b, g, j: (b, k_col(g, cur_item(j)))),
        pl.BlockSpec((seq, dv), lambda b, g, j: (b, v_col(g, prev_item(j)))),
        pl.BlockSpec((ctx_len, HEAD_DIM), lambda b, g, j: (ctx_row0 + b, k_col(g, cur_item(j)))),
        pl.BlockSpec((ctx_len, dv), lambda b, g, j: (ctx_row0 + b, v_col(g, prev_item(j))))]
    scratch = [pltpu.VMEM((n_keys, tq), F32), pltpu.VMEM((n_keys, tq), F32), pltpu.VMEM((2, 8, tq), F32)]
    if diff:
        scratch.append(pltpu.VMEM((dv, tq), F32))
    out_w = dv if diff else HEAD_DIM
    return pl.pallas_call(
        functools.partial(_pipe_kernel, diff=diff, lambda_init=lambda_init),
        out_shape=jax.ShapeDtypeStruct((n_lat, d_out), BF16),
        grid=(n_batch, n_outer, n_items + 1),
        in_specs=in_specs,
        out_specs=pl.BlockSpec((tq, out_w),
                               lambda b, g, j: (b * nq + qblock(prev_item(j)), out_col(g, prev_item(j)))),
        scratch_shapes=scratch,
        compiler_params=_params(("arbitrary", "arbitrary", "arbitrary"),
                                2 * n_keys * tq * 4 + 4 * n_keys * (HEAD_DIM + dv) * 2 + 8 * tq * dv * 4
                                + 8 * PIPE_CHUNK * tq * 4),
        name=name,
    )(*extra_args, y, y, y, y, y)


NB_Q_ROWS = 8
NB_K_ROWS = 16


def _nb_chunks(rows):
    wr = min(WIN_R, rows)
    chunks, patterns = [], []
    for c in range(rows // NB_Q_ROWS):
        kr0 = min(max(NB_Q_ROWS * c - WIN_R // 2, 0), rows - NB_K_ROWS)
        pat = []
        for a in range(NB_K_ROWS):
            for rq in range(NB_Q_ROWS):
                r = NB_Q_ROWS * c + rq
                r0 = min(max(r - WIN_R // 2, 0), rows - wr)
                kr = kr0 + a
                pat.append(kr - r + (WIN_R - 1) if r0 <= kr < r0 + wr else None)
        pat = tuple(pat)
        if pat not in patterns:
            patterns.append(pat)
        chunks.append((kr0, patterns.index(pat)))
    pairs = []
    for pat in patterns:
        for t in range(0, len(pat), 2):
            if pat[t:t + 2] not in pairs:
                pairs.append(pat[t:t + 2])
    return chunks, patterns, pairs


def _nb_kernel(tz_ref, q_ref, k_ref, v_ref, kc_ref, vc_ref, o_ref, bias_ref, *, chunks, patterns, pairs):
    qw = NB_Q_ROWS * GRID_W
    kw = NB_K_ROWS * GRID_W

    @pl.when(pl.program_id(1) == 0)
    def _():
        for p, pat in enumerate(patterns):
            for a in range(NB_K_ROWS):
                for j in range(NB_Q_ROWS // 2):
                    t = a * NB_Q_ROWS + 2 * j
                    bias_ref[p, a * GRID_W:(a + 1) * GRID_W, j * LANES:(j + 1) * LANES] = (
                        tz_ref[0, pairs.index(pat[t:t + 2])])

    kctx = kc_ref[...]
    vctx = vc_ref[...]
    for c, (kr0, p) in enumerate(chunks):
        q = q_ref[c * qw:(c + 1) * qw, :]
        kb = k_ref[kr0 * GRID_W:kr0 * GRID_W + kw, :]
        vb = v_ref[kr0 * GRID_W:kr0 * GRID_W + kw, :]
        dims = (((1,), (1,)), ((), ()))
        s_w = lax.dot_general(kb, q, dims, preferred_element_type=F32) + bias_ref[p]
        s_c = lax.dot_general(kctx, q, dims, preferred_element_type=F32)
        m = jnp.maximum(jnp.max(s_w, axis=0, keepdims=True), jnp.max(s_c, axis=0, keepdims=True))
        p_w = jnp.exp2(s_w - m)
        p_c = jnp.exp2(s_c - m)
        l = jnp.sum(p_w, axis=0, keepdims=True) + jnp.sum(p_c, axis=0, keepdims=True)
        tdims = (((0,), (0,)), ((), ()))
        o_t = (lax.dot_general(vb, p_w.astype(BF16), tdims, preferred_element_type=F32)
               + lax.dot_general(vctx, p_c.astype(BF16), tdims, preferred_element_type=F32))
        o_ref[c * qw:(c + 1) * qw, :] = (o_t / l).T.astype(o_ref.dtype)


def _nb_bias_tiles(rpb, pairs):
    kc = jnp.arange(GRID_W, dtype=jnp.int32)[:, None]
    qc = jnp.arange(GRID_W, dtype=jnp.int32)[None, :]
    c0 = jnp.clip(qc - WIN_C // 2, 0, GRID_W - WIN_C)
    in_win = (kc >= c0) & (kc < c0 + WIN_C)
    dc = jnp.clip(kc - qc + (WIN_C - 1), 0, 2 * WIN_C - 2)
    tiles = jnp.where(in_win[None, None], rpb[:, :, dc] * LOG2E, NEG)
    neg = jnp.full((rpb.shape[0], GRID_W, GRID_W), NEG, F32)
    half = lambda dr: neg if dr is None else tiles[:, dr]
    return jnp.stack([jnp.concatenate([half(e), half(o)], axis=-1) for e, o in pairs], axis=1)


def _nb_attention(y, rpb, n_batch, seq, ctx_len, n_heads):
    rows = seq // GRID_W
    assert seq % GRID_W == 0 and rows % NB_Q_ROWS == 0 and rows >= NB_K_ROWS
    chunks, patterns, pairs = _nb_chunks(rows)
    tz = _nb_bias_tiles(rpb, pairs)
    n_lat = n_batch * seq
    ctx_row0 = n_lat // ctx_len
    qw = NB_Q_ROWS * GRID_W
    kw = NB_K_ROWS * GRID_W
    kern = functools.partial(_nb_kernel, chunks=tuple(chunks), patterns=tuple(patterns), pairs=tuple(pairs))
    return pl.pallas_call(
        kern,
        out_shape=jax.ShapeDtypeStruct((n_lat, n_heads * HEAD_DIM), BF16),
        grid=(n_heads, n_batch),
        in_specs=[pl.BlockSpec((1, len(pairs), GRID_W, 2 * GRID_W), lambda h, b: (h, 0, 0, 0)),
                  pl.BlockSpec((seq, HEAD_DIM), lambda h, b: (b, h)),
                  pl.BlockSpec((seq, HEAD_DIM), lambda h, b: (b, n_heads + h)),
                  pl.BlockSpec((seq, HEAD_DIM), lambda h, b: (b, 2 * n_heads + h)),
                  pl.BlockSpec((ctx_len, HEAD_DIM), lambda h, b: (ctx_row0 + b, n_heads + h)),
                  pl.BlockSpec((ctx_len, HEAD_DIM), lambda h, b: (ctx_row0 + b, 2 * n_heads + h))],
        out_specs=pl.BlockSpec((seq, HEAD_DIM), lambda h, b: (b, h)),
        scratch_shapes=[pltpu.VMEM((len(patterns), kw, qw), F32)],
        compiler_params=_params(("arbitrary", "arbitrary"),
                                8 * seq * HEAD_DIM * 2 + len(patterns) * kw * qw * 4
                                + 2 * len(pairs) * GRID_W * LANES * 4 + 6 * (kw + ctx_len) * qw * 4),
        name="attn_nb",
    )(tz, y, y, y, y, y)


def _rope_tables(seq, pad_rows):
    t = jnp.arange(seq, dtype=jnp.int32)
    row = (t // GRID_W).astype(F32)
    col = (t % GRID_W).astype(F32)
    n_freq = HEAD_DIM // 4
    inv_freq = ROPE_THETA ** (-jnp.arange(n_freq, dtype=F32) / n_freq)
    ang = jnp.concatenate([row[:, None] * inv_freq, col[:, None] * inv_freq], axis=-1)
    cos, sin = jnp.cos(ang), jnp.sin(ang)
    cos2 = jnp.concatenate([cos, cos], axis=-1)
    sin2 = jnp.concatenate([-sin, sin], axis=-1)
    cos2 = jnp.concatenate([cos2, jnp.ones((pad_rows, HEAD_DIM), F32)], axis=0)
    sin2 = jnp.concatenate([sin2, jnp.zeros((pad_rows, HEAD_DIM), F32)], axis=0)
    return cos2, sin2


def _split_pairs(w, n_cols):
    k = w.shape[0]
    a = w[:, :n_cols].reshape(k, n_cols // HEAD_DIM, HEAD_DIM // 2, 2)
    a = jnp.swapaxes(a, -1, -2).reshape(k, n_cols)
    return jnp.concatenate([a, w[:, n_cols:]], axis=-1)


def kernel(x, c, ctx, c_ctx, norm1_g, norm2_g, mod_down, mod_up, mod_b, mlp_w1, mlp_w2, a_w_qkv, a_w_o, a_q_g, a_k_g, b_w_qkv, b_w_o, b_rpb, c_w_qkv, c_w_o, c_lam_q1, c_lam_k1, c_lam_q2, c_lam_k2, c_subln_g, final_g):
    n_batch, seq, d = x.shape
    ctx_len = ctx.shape[1]
    depth = norm1_g.shape[0]
    n_lat = n_batch * seq
    n_all = n_lat + n_batch * ctx_len
    n_mixers = 3
    heads = d // HEAD_DIM

    h = jnp.concatenate([x.reshape(n_lat, d), ctx.reshape(n_batch * ctx_len, d)], axis=0)
    cond = jnp.concatenate([c, c_ctx[None], jnp.zeros((MOD_ROWS - n_batch - 1, d), F32)], axis=0)
    mods = _modulation(cond, mod_down, mod_up, mod_b)
    tm = _row_tile(seq, n_batch * ctx_len)
    cos, sin = _rope_tables(seq, tm)
    ones_g = jnp.ones((1, HEAD_DIM), F32)

    for i in range(depth):
        with_ctx_out = i < depth - 1
        n_rows = n_all if with_ctx_out else n_lat
        mod = mods[i]
        kind, slot = i % n_mixers, i // n_mixers
        u = _norm_mod(h, norm1_g[i], mod, 0, 1, n_all, seq, n_batch, BF16)
        if kind == 0:
            kv_heads = heads // 4
            n_q, n_kv = heads * HEAD_DIM, kv_heads * HEAD_DIM
            w = _split_pairs(a_w_qkv[slot], n_q + n_kv).astype(BF16)
            gq = _split_pairs(a_q_g[slot][None], HEAD_DIM)
            gk = _split_pairs(a_k_g[slot][None], HEAD_DIM)
            y = _mm_qkv(u, w, cos, sin, gq, gk, "A", n_q, n_kv, seq, n_batch, tm)
            nq = seq // PIPE_TQ
            o = _pipe_attention(y, d, n_batch, seq, ctx_len, kv_heads, 1, 4, HEAD_DIM,
                                lambda g, it: g * 4 + it // nq, lambda g, it: heads + g,
                                lambda g, it: heads + kv_heads + g, lambda g, it: g * 4 + it // nq,
                                (), (), None, "attn_lat")
            if with_ctx_out:
                o_c = _gqa_attention(y, n_batch, seq, ctx_len, heads, 4, heads, heads + kv_heads, False)
            w_o = a_w_o[slot]
        elif kind == 1:
            y = _mm_qkv(u, b_w_qkv[slot].astype(BF16), cos, sin, ones_g, ones_g, "B", d, d, seq, n_batch, tm)
            o = _nb_attention(y, b_rpb[slot], n_batch, seq, ctx_len, heads)
            if with_ctx_out:
                o_c = _gqa_attention(y, n_batch, seq, ctx_len, heads, 1, heads, 2 * heads, False)
            w_o = b_w_o[slot]
        else:
            lambda_init = 0.8 - 0.6 * math.exp(-0.3 * i)
            w = _split_pairs(c_w_qkv[slot], 2 * d).astype(BF16)
            y = _mm_qkv(u, w, cos, sin, ones_g, ones_g, "C", d, d, seq, n_batch, tm)
            lam_vecs = jnp.concatenate([c_lam_q1[slot][None], c_lam_k1[slot][None], c_lam_q2[slot][None],
                                        c_lam_k2[slot][None], jnp.zeros((4, HEAD_DIM), F32)], axis=0)
            dh = 2 * HEAD_DIM
            o = _pipe_attention(y, d, n_batch, seq, ctx_len, heads // 2, 2, 1, dh,
                                lambda g, it: 2 * g + it % 2, lambda g, it: heads + 2 * g + it % 2,
                                lambda g, it: heads + g, lambda g, it: g,
                                (lam_vecs, c_subln_g[slot].reshape(1, dh)),
                                (pl.BlockSpec((8, HEAD_DIM), lambda b, g, j: (0, 0)),
                                 pl.BlockSpec((1, dh), lambda b, g, j: (0, 0))),
                                lambda_init, "diff_lat")
            if with_ctx_out:
                o_c = _diff_attention(y, lam_vecs, c_subln_g[slot], lambda_init, n_batch, seq, ctx_len,
                                      heads // 2, False)
            w_o = c_w_o[slot]
        if with_ctx_out:
            o = jnp.concatenate([o, o_c], axis=0)
        h = _mm_resid(o, w_o.astype(BF16), h, mod, 2, n_rows, seq, n_batch, tm)
        v = _norm_mod(h, norm2_g[i], mod, 3, 4, n_rows, seq, n_batch, BF16)
        a = _mm_act(v, mlp_w1[i].astype(BF16), n_rows, True, tm)
        h = _mm_resid(a, mlp_w2[i].astype(BF16), h, mod, 5, n_rows, seq, n_batch, tm)

    zeros_mod = jnp.zeros((MOD_ROWS, N_MOD * d), F32)
    out = _norm_mod(h, final_g, zeros_mod, 0, 1, n_lat, seq, n_batch, F32)
    return out.reshape(n_batch, seq, d)
```

```python
import functools
import math

import jax
import jax.numpy as jnp
from jax import lax
from jax.experimental import pallas as pl
from jax.experimental.pallas import tpu as pltpu

F32 = jnp.float32
BF16 = jnp.bfloat16

LANES = 128
V7X_VMEM_BYTES = 64 * 2**20
VMEM_RESERVE_BYTES = 6 * 2**20

HEAD_DIM = 128
GRID_W = 64
WIN_R = 8
WIN_C = 16
MOD_ROWS = 8
N_MOD = 6
ROPE_THETA = 10000.0
EPS = 1e-6
LOG2E = 1.4426950408889634
NEG = -1e30
KEY_CHUNK = 512


def _vmem_limit(block_bytes):
    return int(min(block_bytes + VMEM_RESERVE_BYTES, V7X_VMEM_BYTES - 2**20))


def _params(sem, block_bytes):
    return pltpu.CompilerParams(dimension_semantics=sem, vmem_limit_bytes=_vmem_limit(block_bytes))


def _row_tile(seq, n_ctx_rows, cap=1024):
    tm = cap
    while seq % tm or n_ctx_rows % tm:
        tm //= 2
    return tm


def _mod_row(tile, tm, seq, n_batch):
    start = tile * tm
    return jnp.where(start < n_batch * seq, start // seq, n_batch)


def _mod_down_kernel(c_ref, w_ref, o_ref):
    c = c_ref[...]
    s = c * jax.nn.sigmoid(c)
    o_ref[0] = jnp.dot(s.astype(BF16), w_ref[0].astype(BF16), preferred_element_type=F32)


def _mod_up_kernel(t_ref, w_ref, b_ref, o_ref):
    o_ref[0] = jnp.dot(t_ref[0].astype(BF16), w_ref[0].astype(BF16),
                       preferred_element_type=F32) + b_ref[0]


def _modulation(cond, mod_down, mod_up, mod_b):
    depth, d, rank = mod_down.shape
    n_out = mod_up.shape[2]
    tr = 512
    t = pl.pallas_call(
        _mod_down_kernel,
        out_shape=jax.ShapeDtypeStruct((depth, MOD_ROWS, rank), F32),
        grid=(depth, rank // tr),
        in_specs=[pl.BlockSpec((MOD_ROWS, d), lambda l, j: (0, 0)),
                  pl.BlockSpec((1, d, tr), lambda l, j: (l, 0, j))],
        out_specs=pl.BlockSpec((1, MOD_ROWS, tr), lambda l, j: (l, 0, j)),
        compiler_params=_params(("arbitrary", "arbitrary"), 2 * d * tr * 4 + d * tr * 2 + 4 * MOD_ROWS * d * 4),
        name="mod_down",
    )(cond, mod_down)
    tn = 2048
    return pl.pallas_call(
        _mod_up_kernel,
        out_shape=jax.ShapeDtypeStruct((depth, MOD_ROWS, n_out), F32),
        grid=(depth, n_out // tn),
        in_specs=[pl.BlockSpec((1, MOD_ROWS, rank), lambda l, j: (l, 0, 0)),
                  pl.BlockSpec((1, rank, tn), lambda l, j: (l, 0, j)),
                  pl.BlockSpec((1, 1, tn), lambda l, j: (l, 0, j))],
        out_specs=pl.BlockSpec((1, MOD_ROWS, tn), lambda l, j: (l, 0, j)),
        compiler_params=_params(("arbitrary", "arbitrary"), 2 * rank * tn * 4 + rank * tn * 2),
        name="mod_up",
    )(t, mod_up, mod_b.reshape(depth, 1, n_out))


def _norm_mod_kernel(h_ref, g_ref, sh_ref, sc_ref, o_ref, *, tm, seq, n_batch):
    row = _mod_row(pl.program_id(0), tm, seq, n_batch)
    x = h_ref[...]
    y = x * lax.rsqrt(jnp.mean(x * x, axis=-1, keepdims=True) + EPS)
    y = y * g_ref[...]
    o_ref[...] = (y * (1.0 + sc_ref[pl.ds(row, 1), :]) + sh_ref[pl.ds(row, 1), :]).astype(o_ref.dtype)


def _norm_mod(h, gain, mod, shift_chunk, scale_chunk, n_rows, seq, n_batch, out_dtype):
    d = h.shape[1]
    tm = 256
    kern = functools.partial(_norm_mod_kernel, tm=tm, seq=seq, n_batch=n_batch)
    return pl.pallas_call(
        kern,
        out_shape=jax.ShapeDtypeStruct((n_rows, d), out_dtype),
        grid=(n_rows // tm,),
        in_specs=[pl.BlockSpec((tm, d), lambda i: (i, 0)),
                  pl.BlockSpec((1, d), lambda i: (0, 0)),
                  pl.BlockSpec((MOD_ROWS, d), lambda i: (0, shift_chunk)),
                  pl.BlockSpec((MOD_ROWS, d), lambda i: (0, scale_chunk))],
        out_specs=pl.BlockSpec((tm, d), lambda i: (i, 0)),
        compiler_params=_params(("arbitrary",), 2 * tm * d * 4 + 2 * tm * d * 4 + 3 * tm * d * 4),
        name="norm_mod",
    )(h, gain.reshape(1, d), mod, mod)


def _mm_act_kernel(x_ref, w_ref, o_ref, *, relu2):
    acc = jnp.dot(x_ref[...], w_ref[...], preferred_element_type=F32)
    if relu2:
        acc = jnp.square(jnp.maximum(acc, 0.0))
    o_ref[...] = acc.astype(o_ref.dtype)


def _mm_act(x, w, n_rows, relu2, tm):
    k = x.shape[1]
    n = w.shape[1]
    tn = 1024
    return pl.pallas_call(
        functools.partial(_mm_act_kernel, relu2=relu2),
        out_shape=jax.ShapeDtypeStruct((n_rows, n), BF16),
        grid=(n_rows // tm, n // tn),
        in_specs=[pl.BlockSpec((tm, k), lambda i, j: (i, 0)),
                  pl.BlockSpec((k, tn), lambda i, j: (0, j))],
        out_specs=pl.BlockSpec((tm, tn), lambda i, j: (i, j)),
        compiler_params=_params(("arbitrary", "arbitrary"),
                                2 * tm * k * 2 + 2 * k * tn * 2 + 2 * tm * tn * 2 + 2 * tm * tn * 4),
        name="mm_relu2" if relu2 else "mm_plain",
    )(x, w)


def _mm_resid_kernel(*refs, tm, seq, n_batch, nk, n_lat_tiles):
    if n_lat_tiles is None:
        x_ref, w_ref, h_ref, g_ref, o_ref, acc_ref = refs
    else:
        x_ref, xc_ref, w_ref, h_ref, g_ref, o_ref, acc_ref = refs
    i = pl.program_id(0)
    kk = pl.program_id(2)

    @pl.when(kk == 0)
    def _():
        acc_ref[...] = jnp.zeros_like(acc_ref)

    if n_lat_tiles is None:
        acc_ref[...] += jnp.dot(x_ref[...], w_ref[...], preferred_element_type=F32)
    else:
        @pl.when(i < n_lat_tiles)
        def _():
            acc_ref[...] += jnp.dot(x_ref[...], w_ref[...], preferred_element_type=F32)

        @pl.when(i >= n_lat_tiles)
        def _():
            acc_ref[...] += jnp.dot(xc_ref[...], w_ref[...], preferred_element_type=F32)

    @pl.when(kk == nk - 1)
    def _():
        row = _mod_row(i, tm, seq, n_batch)
        o_ref[...] = h_ref[...] + g_ref[pl.ds(row, 1), :] * acc_ref[...]


def _mm_resid(x, x_ctx, w, h, mod, gate_chunk, n_rows, seq, n_batch, tm):
    k = x.shape[1]
    n = w.shape[1]
    tn, tk = 1024, min(k, 2048)
    nk = k // tk
    gate_blk = gate_chunk * (n // tn)
    n_lat_tiles = None if x_ctx is None else x.shape[0] // tm
    kern = functools.partial(_mm_resid_kernel, tm=tm, seq=seq, n_batch=n_batch, nk=nk, n_lat_tiles=n_lat_tiles)
    if x_ctx is None:
        lhs = [x]
        lhs_specs = [pl.BlockSpec((tm, tk), lambda i, j, kk: (i, kk))]
    else:
        lhs = [x, x_ctx]
        lhs_specs = [pl.BlockSpec((tm, tk), lambda i, j, kk: (jnp.minimum(i, n_lat_tiles - 1),
                                                              jnp.where(i < n_lat_tiles, kk, nk - 1))),
                     pl.BlockSpec((tm, tk), lambda i, j, kk: (jnp.maximum(i - n_lat_tiles, 0),
                                                              jnp.where(i < n_lat_tiles, 0, kk)))]
    return pl.pallas_call(
        kern,
        out_shape=jax.ShapeDtypeStruct((n_rows, n), F32),
        grid=(n_rows // tm, n // tn, nk),
        in_specs=lhs_specs + [pl.BlockSpec((tk, tn), lambda i, j, kk: (kk, j)),
                              pl.BlockSpec((tm, tn), lambda i, j, kk: (i, j)),
                              pl.BlockSpec((MOD_ROWS, tn), lambda i, j, kk: (0, gate_blk + j))],
        out_specs=pl.BlockSpec((tm, tn), lambda i, j, kk: (i, j)),
        scratch_shapes=[pltpu.VMEM((tm, tn), F32)],
        compiler_params=_params(("arbitrary", "arbitrary", "arbitrary"),
                                2 * len(lhs) * tm * tk * 2 + 2 * tk * tn * 2 + 4 * tm * tn * 4 + 2 * tm * tn * 4),
        name="mm_resid",
    )(*lhs, w, h, mod)


QKV_COL_CHUNK = 512


def _qkv_kernel(x_ref, w_ref, cos_ref, sin_ref, gq_ref, gk_ref, o_ref, *, kind, nq, nk, qscale):
    j = pl.program_id(1)

    def norm(x, g_ref):
        return x * lax.rsqrt(jnp.mean(x * x, axis=-1, keepdims=True) + EPS) * g_ref[...]

    def rope(x):
        return x * cos_ref[...] + pltpu.roll(x, HEAD_DIM // 2, 1) * sin_ref[...]

    chunk = QKV_COL_CHUNK if kind == "C" else o_ref.shape[1]

    def run(fn):
        for c0 in range(0, o_ref.shape[1], chunk):
            acc = jnp.dot(x_ref[...], w_ref[:, c0:c0 + chunk], preferred_element_type=F32)
            if fn is None:
                o_ref[:, c0:c0 + chunk] = acc.astype(o_ref.dtype)
                continue
            for h0 in range(0, chunk, HEAD_DIM):
                o_ref[:, c0 + h0:c0 + h0 + HEAD_DIM] = fn(acc[:, h0:h0 + HEAD_DIM]).astype(o_ref.dtype)

    if kind == "A":
        q_fn = lambda x: rope(norm(x, gq_ref)) * qscale
        k_fn = lambda x: rope(norm(x, gk_ref))
    elif kind == "C":
        q_fn = lambda x: rope(x) * qscale
        k_fn = rope
    else:
        q_fn = lambda x: x * qscale
        k_fn = None

    @pl.when(j < nq)
    def _():
        run(q_fn)

    if k_fn is None:
        @pl.when(j >= nq)
        def _():
            run(None)
    else:
        @pl.when((j >= nq) & (j < nq + nk))
        def _():
            run(k_fn)

        @pl.when(j >= nq + nk)
        def _():
            run(None)


def _mm_qkv(x, w, rope_tabs, gq, gk, kind, n_q_cols, n_k_cols, seq, n_batch, tm):
    n_rows, k = x.shape
    n = w.shape[1]
    tn = 1024
    n_lat_tiles = n_batch * seq // tm
    tiles_per_seq = seq // tm
    qscale = HEAD_DIM ** -0.5 * LOG2E

    def rope_map(i, j):
        return (jnp.where(i < n_lat_tiles, i % tiles_per_seq, tiles_per_seq), 0)

    kern = functools.partial(_qkv_kernel, kind=kind, nq=n_q_cols // tn, nk=n_k_cols // tn, qscale=qscale)
    return pl.pallas_call(
        kern,
        out_shape=jax.ShapeDtypeStruct((n_rows, n), BF16),
        grid=(n_rows // tm, n // tn),
        in_specs=[pl.BlockSpec((tm, k), lambda i, j: (i, 0)),
                  pl.BlockSpec((k, tn), lambda i, j: (0, j)),
                  pl.BlockSpec((tm, HEAD_DIM), rope_map),
                  pl.BlockSpec((tm, HEAD_DIM), rope_map),
                  pl.BlockSpec((1, HEAD_DIM), lambda i, j: (0, 0)),
                  pl.BlockSpec((1, HEAD_DIM), lambda i, j: (0, 0))],
        out_specs=pl.BlockSpec((tm, tn), lambda i, j: (i, j)),
        compiler_params=_params(("arbitrary", "arbitrary"),
                                2 * tm * k * 2 + 2 * k * tn * 2 + 2 * tm * tn * 2 + 2 * tm * tn * 4
                                + 4 * tm * HEAD_DIM * 4),
        name="mm_qkv_" + kind,
    )(x, w, *rope_tabs, gq, gk)


def _softmax_step(state, k_blk, v_blk, q):
    m, l, acc = state
    s = lax.dot_general(k_blk, q, (((1,), (1,)), ((), ())), preferred_element_type=F32)
    m_new = jnp.maximum(m, jnp.max(s, axis=0, keepdims=True))
    alpha = jnp.exp2(m - m_new)
    p = jnp.exp2(s - m_new)
    l = alpha * l + jnp.sum(p, axis=0, keepdims=True)
    pv = lax.dot_general(v_blk, p.astype(BF16), (((0,), (0,)), ((), ())), preferred_element_type=F32)
    return m_new, l, alpha * acc + pv


def _softmax_init(tq, dv):
    return (jnp.full((1, tq), NEG, F32), jnp.zeros((1, tq), F32), jnp.zeros((dv, tq), F32))


def _attend(q, kv_refs, dk_slice):
    tq = q.shape[0]
    dv = kv_refs[0][1].shape[1]
    state = _softmax_init(tq, dv)
    for k_ref, v_ref in kv_refs:
        n_keys = k_ref.shape[0]
        step = min(KEY_CHUNK, n_keys)
        for c in range(n_keys // step):
            rows = slice(c * step, (c + 1) * step)
            state = _softmax_step(state, k_ref[rows, dk_slice], v_ref[rows, :], q)
    m, l, acc = state
    return acc / l


def _gqa_kernel(q_ref, *refs):
    o_ref = refs[-1]
    kv = [(refs[i], refs[i + 1]) for i in range(0, len(refs) - 1, 2)]
    o_t = _attend(q_ref[...], kv, slice(None))
    o_ref[...] = o_t.T.astype(o_ref.dtype)


def _gqa_attention(y, n_batch, seq, ctx_len, n_heads, group, k_col0, v_col0, latent):
    n_lat = n_batch * seq
    d_out = n_heads * HEAD_DIM
    if latent:
        tq = 512
        nq = seq // tq
        q_map = lambda b, h, i: (b * nq + i, h)
        n_out = n_lat
    else:
        tq = ctx_len
        nq = 1
        q_map = lambda b, h, i: (n_lat // ctx_len + b, h)
        n_out = n_batch * ctx_len
    ctx_row0 = n_lat // ctx_len
    in_specs = [pl.BlockSpec((tq, HEAD_DIM), q_map)]
    args = [y]
    blk = 2 * tq * HEAD_DIM * 2 * 2
    if latent:
        in_specs += [pl.BlockSpec((seq, HEAD_DIM), lambda b, h, i: (b, k_col0 + h // group)),
                     pl.BlockSpec((seq, HEAD_DIM), lambda b, h, i: (b, v_col0 + h // group))]
        args += [y, y]
        blk += 4 * seq * HEAD_DIM * 2
    in_specs += [pl.BlockSpec((ctx_len, HEAD_DIM), lambda b, h, i: (ctx_row0 + b, k_col0 + h // group)),
                 pl.BlockSpec((ctx_len, HEAD_DIM), lambda b, h, i: (ctx_row0 + b, v_col0 + h // group))]
    args += [y, y]
    if latent:
        out_map = lambda b, h, i: (b * nq + i, h)
    else:
        out_map = lambda b, h, i: (b, h)
    return pl.pallas_call(
        _gqa_kernel,
        out_shape=jax.ShapeDtypeStruct((n_out, d_out), BF16),
        grid=(n_batch, n_heads, nq),
        in_specs=in_specs,
        out_specs=pl.BlockSpec((tq, HEAD_DIM), out_map),
        compiler_params=_params(("arbitrary", "arbitrary", "arbitrary"), blk + 8 * KEY_CHUNK * tq * 4),
        name="attn_lat" if latent else "attn_ctx",
    )(*args)


def _diff_kernel(lam_ref, g_ref, q_ref, *refs, lambda_init):
    o_ref = refs[-1]
    kv = [(refs[i], refs[i + 1]) for i in range(0, len(refs) - 1, 2)]
    lq1, lk1, lq2, lk2 = (lam_ref[r:r + 1, :] for r in range(4))
    lam = (jnp.exp(jnp.sum(lq1 * lk1, axis=-1, keepdims=True))
           - jnp.exp(jnp.sum(lq2 * lk2, axis=-1, keepdims=True)) + lambda_init)
    q = q_ref[...]
    o0 = _attend(q[:, :HEAD_DIM], kv, slice(0, HEAD_DIM))
    o1 = _attend(q[:, HEAD_DIM:], kv, slice(HEAD_DIM, 2 * HEAD_DIM))
    o = (o0 - lam * o1).T
    o = o * lax.rsqrt(jnp.mean(o * o, axis=-1, keepdims=True) + EPS) * g_ref[...]
    o_ref[...] = (o * (1.0 - lambda_init)).astype(o_ref.dtype)


def _diff_attention(y, lam_vecs, subln_g, lambda_init, n_batch, seq, ctx_len, n_heads, latent):
    dh = 2 * HEAD_DIM
    n_lat = n_batch * seq
    ctx_row0 = n_lat // ctx_len
    if latent:
        tq = 512
        nq = seq // tq
        q_map = lambda b, h, i: (b * nq + i, h)
        out_map = q_map
        n_out = n_lat
    else:
        tq = ctx_len
        nq = 1
        q_map = lambda b, h, i: (ctx_row0 + b, h)
        out_map = lambda b, h, i: (b, h)
        n_out = n_batch * ctx_len
    in_specs = [pl.BlockSpec((8, HEAD_DIM), lambda b, h, i: (0, 0)),
                pl.BlockSpec((1, dh), lambda b, h, i: (0, 0)),
                pl.BlockSpec((tq, dh), q_map)]
    args = [lam_vecs, subln_g.reshape(1, dh), y]
    blk = 2 * tq * dh * 2 * 2
    if latent:
        in_specs += [pl.BlockSpec((seq, dh), lambda b, h, i: (b, n_heads + h)),
                     pl.BlockSpec((seq, dh), lambda b, h, i: (b, 2 * n_heads + h))]
        args += [y, y]
        blk += 4 * seq * dh * 2
    in_specs += [pl.BlockSpec((ctx_len, dh), lambda b, h, i: (ctx_row0 + b, n_heads + h)),
                 pl.BlockSpec((ctx_len, dh), lambda b, h, i: (ctx_row0 + b, 2 * n_heads + h))]
    args += [y, y]
    return pl.pallas_call(
        functools.partial(_diff_kernel, lambda_init=lambda_init),
        out_shape=jax.ShapeDtypeStruct((n_out, n_heads * dh), BF16),
        grid=(n_batch, n_heads, nq),
        in_specs=in_specs,
        out_specs=pl.BlockSpec((tq, dh), out_map),
        compiler_params=_params(("arbitrary", "arbitrary", "arbitrary"), blk + 10 * KEY_CHUNK * tq * 4),
        name="diff_lat" if latent else "diff_ctx",
    )(*args)


PIPE_TQ = 512
PIPE_CHUNK = 256


def _pipe_kernel(*refs, diff, lambda_init):
    if diff:
        lam_ref, g_ref, q_ref, k_ref, v_ref, kc_ref, vc_ref, o_ref, sa_ref, sb_ref, m_ref, o0_ref = refs
    else:
        q_ref, k_ref, v_ref, kc_ref, vc_ref, o_ref, sa_ref, sb_ref, m_ref = refs
    j = pl.program_id(2)
    tq = q_ref.shape[0]
    dv = v_ref.shape[1]

    @pl.when(j == 0)
    def _():
        sb_ref[...] = jnp.zeros(sb_ref.shape, F32)
        m_ref[1] = jnp.zeros(m_ref.shape[1:], F32)
        if diff:
            o0_ref[...] = jnp.zeros(o0_ref.shape, F32)

    def body(s_cur, s_prev, cur, prev):
        q = q_ref[...]
        m_new = jnp.full((1, tq), NEG, F32)
        m_old = m_ref[prev][0:1, :]
        l = jnp.zeros((1, tq), F32)
        acc = jnp.zeros((dv, tq), F32)
        blocks = [(k_ref, v_ref, c * PIPE_CHUNK, PIPE_CHUNK) for c in range(k_ref.shape[0] // PIPE_CHUNK)]
        blocks.append((kc_ref, vc_ref, 0, kc_ref.shape[0]))
        off = 0
        for kr, vr, r0, n in blocks:
            s = lax.dot_general(kr[r0:r0 + n, :], q, (((1,), (1,)), ((), ())), preferred_element_type=F32)
            s_cur[off:off + n, :] = s
            m_new = jnp.maximum(m_new, jnp.max(s, axis=0, keepdims=True))
            p = jnp.exp2(s_prev[off:off + n, :] - m_old)
            l = l + jnp.sum(p, axis=0, keepdims=True)
            acc = acc + lax.dot_general(vr[r0:r0 + n, :], p.astype(BF16), (((0,), (0,)), ((), ())),
                                        preferred_element_type=F32)
            off += n
        m_ref[cur] = jnp.broadcast_to(m_new, m_ref.shape[1:])
        o_t = acc / l
        if not diff:
            o_ref[...] = o_t.T.astype(o_ref.dtype)
        elif cur == 1:
            o0_ref[...] = o_t
        else:
            lq1, lk1, lq2, lk2 = (lam_ref[r:r + 1, :] for r in range(4))
            lam = (jnp.exp(jnp.sum(lq1 * lk1, axis=-1, keepdims=True))
                   - jnp.exp(jnp.sum(lq2 * lk2, axis=-1, keepdims=True)) + lambda_init)
            o = (o0_ref[...] - lam * o_t).T
            o = o * lax.rsqrt(jnp.mean(o * o, axis=-1, keepdims=True) + EPS) * g_ref[...]
            o_ref[...] = (o * (1.0 - lambda_init)).astype(o_ref.dtype)

    @pl.when(j % 2 == 0)
    def _():
        body(sa_ref, sb_ref, 0, 1)

    @pl.when(j % 2 == 1)
    def _():
        body(sb_ref, sa_ref, 1, 0)


def _pipe_attention(y, d_out, n_batch, seq, ctx_len, n_outer, items_per_qblock, heads_per_outer, dv,
                    q_col, k_col, v_col, out_col, extra_args, extra_specs, lambda_init, name):
    diff = lambda_init is not None
    n_lat = n_batch * seq
    ctx_row0 = n_lat // ctx_len
    tq = PIPE_TQ
    nq = seq // tq
    n_items = nq * items_per_qblock * heads_per_outer
    n_keys = seq + ctx_len

    def qblock(it):
        return (it // items_per_qblock) % nq

    def cur_item(j):
        return jnp.minimum(j, n_items - 1)

    def prev_item(j):
        return jnp.maximum(j - 1, 0)

    in_specs = list(extra_specs) + [
        pl.BlockSpec((tq, HEAD_DIM), lambda b, g, j: (b * nq + qblock(cur_item(j)), q_col(g, cur_item(j)))),
        pl.BlockSpec((seq, HEAD_DIM), lambda b, g, j: (b, k_col(g, cur_item(j)))),
        pl.BlockSpec((seq, dv), lambda b, g, j: (b, v_col(g, prev_item(j)))),
        pl.BlockSpec((ctx_len, HEAD_DIM), lambda b, g, j: (ctx_row0 + b, k_col(g, cur_item(j)))),
        pl.BlockSpec((ctx_len, dv), lambda b, g, j: (ctx_row0 + b, v_col(g, prev_item(j))))]
    scratch = [pltpu.VMEM((n_keys, tq), F32), pltpu.VMEM((n_keys, tq), F32), pltpu.VMEM((2, 8, tq), F32)]
    if diff:
        scratch.append(pltpu.VMEM((dv, tq), F32))
    out_w = dv if diff else HEAD_DIM
    return pl.pallas_call(
        functools.partial(_pipe_kernel, diff=diff, lambda_init=lambda_init),
        out_shape=jax.ShapeDtypeStruct((n_lat, d_out), BF16),
        grid=(n_batch, n_outer, n_items + 1),
        in_specs=in_specs,
        out_specs=pl.BlockSpec((tq, out_w),
                               lambda b, g, j: (b * nq + qblock(prev_item(j)), out_col(g, prev_item(j)))),
        scratch_shapes=scratch,
        compiler_params=_params(("arbitrary", "arbitrary", "arbitrary"),
                                2 * n_keys * tq * 4 + 4 * n_keys * (HEAD_DIM + dv) * 2 + 8 * tq * dv * 4
                                + 8 * PIPE_CHUNK * tq * 4),
        name=name,
    )(*extra_args, y, y, y, y, y)


NB_Q_ROWS = 8
NB_K_ROWS = 16


def _nb_chunks(rows):
    wr = min(WIN_R, rows)
    chunks, patterns = [], []
    for c in range(rows // NB_Q_ROWS):
        kr0 = min(max(NB_Q_ROWS * c - WIN_R // 2, 0), rows - NB_K_ROWS)
        pat = []
        for a in range(NB_K_ROWS):
            for rq in range(NB_Q_ROWS):
                r = NB_Q_ROWS * c + rq
                r0 = min(max(r - WIN_R // 2, 0), rows - wr)
                kr = kr0 + a
                pat.append(kr - r + (WIN_R - 1) if r0 <= kr < r0 + wr else None)
        pat = tuple(pat)
        if pat not in patterns:
            patterns.append(pat)
        chunks.append((kr0, patterns.index(pat)))
    pairs = []
    for pat in patterns:
        for t in range(0, len(pat), 2):
            if pat[t:t + 2] not in pairs:
                pairs.append(pat[t:t + 2])
    return chunks, patterns, pairs


def _nb_kernel(tz_ref, q_ref, k_ref, v_ref, kc_ref, vc_ref, o_ref, bias_ref, *, chunks, patterns, pairs):
    qw = NB_Q_ROWS * GRID_W
    kw = NB_K_ROWS * GRID_W

    @pl.when(pl.program_id(1) == 0)
    def _():
        for p, pat in enumerate(patterns):
            for a in range(NB_K_ROWS):
                for j in range(NB_Q_ROWS // 2):
                    t = a * NB_Q_ROWS + 2 * j
                    bias_ref[p, a * GRID_W:(a + 1) * GRID_W, j * LANES:(j + 1) * LANES] = (
                        tz_ref[0, pairs.index(pat[t:t + 2])])

    kctx = kc_ref[...]
    vctx = vc_ref[...]

    def scores(c):
        kr0, p = chunks[c]
        q = q_ref[c * qw:(c + 1) * qw, :]
        kb = k_ref[kr0 * GRID_W:kr0 * GRID_W + kw, :]
        dims = (((1,), (1,)), ((), ()))
        s_w = lax.dot_general(kb, q, dims, preferred_element_type=F32) + bias_ref[p]
        s_c = lax.dot_general(kctx, q, dims, preferred_element_type=F32)
        return s_w, s_c

    def finish(c, s_w, s_c):
        kr0 = chunks[c][0]
        vb = v_ref[kr0 * GRID_W:kr0 * GRID_W + kw, :]
        m = jnp.maximum(jnp.max(s_w, axis=0, keepdims=True), jnp.max(s_c, axis=0, keepdims=True))
        p_w = jnp.exp2(s_w - m)
        p_c = jnp.exp2(s_c - m)
        l = jnp.sum(p_w, axis=0, keepdims=True) + jnp.sum(p_c, axis=0, keepdims=True)
        tdims = (((0,), (0,)), ((), ()))
        o_t = (lax.dot_general(vb, p_w.astype(BF16), tdims, preferred_element_type=F32)
               + lax.dot_general(vctx, p_c.astype(BF16), tdims, preferred_element_type=F32))
        o_ref[c * qw:(c + 1) * qw, :] = (o_t / l).T.astype(o_ref.dtype)

    pending = scores(0)
    for c in range(1, len(chunks)):
        upcoming = scores(c)
        finish(c - 1, *pending)
        pending = upcoming
    finish(len(chunks) - 1, *pending)


def _nb_bias_tiles(rpb, pairs):
    kc = jnp.arange(GRID_W, dtype=jnp.int32)[:, None]
    qc = jnp.arange(GRID_W, dtype=jnp.int32)[None, :]
    c0 = jnp.clip(qc - WIN_C // 2, 0, GRID_W - WIN_C)
    in_win = (kc >= c0) & (kc < c0 + WIN_C)
    dc = jnp.clip(kc - qc + (WIN_C - 1), 0, 2 * WIN_C - 2)
    tiles = jnp.where(in_win[None, None], rpb[:, :, dc] * LOG2E, NEG)
    neg = jnp.full((rpb.shape[0], GRID_W, GRID_W), NEG, F32)
    half = lambda dr: neg if dr is None else tiles[:, dr]
    return jnp.stack([jnp.concatenate([half(e), half(o)], axis=-1) for e, o in pairs], axis=1)


def _nb_attention(y, rpb, n_batch, seq, ctx_len, n_heads):
    rows = seq // GRID_W
    assert seq % GRID_W == 0 and rows % NB_Q_ROWS == 0 and rows >= NB_K_ROWS
    chunks, patterns, pairs = _nb_chunks(rows)
    tz = _nb_bias_tiles(rpb, pairs)
    n_lat = n_batch * seq
    ctx_row0 = n_lat // ctx_len
    qw = NB_Q_ROWS * GRID_W
    kw = NB_K_ROWS * GRID_W
    kern = functools.partial(_nb_kernel, chunks=tuple(chunks), patterns=tuple(patterns), pairs=tuple(pairs))
    return pl.pallas_call(
        kern,
        out_shape=jax.ShapeDtypeStruct((n_lat, n_heads * HEAD_DIM), BF16),
        grid=(n_heads, n_batch),
        in_specs=[pl.BlockSpec((1, len(pairs), GRID_W, 2 * GRID_W), lambda h, b: (h, 0, 0, 0)),
                  pl.BlockSpec((seq, HEAD_DIM), lambda h, b: (b, h)),
                  pl.BlockSpec((seq, HEAD_DIM), lambda h, b: (b, n_heads + h)),
                  pl.BlockSpec((seq, HEAD_DIM), lambda h, b: (b, 2 * n_heads + h)),
                  pl.BlockSpec((ctx_len, HEAD_DIM), lambda h, b: (ctx_row0 + b, n_heads + h)),
                  pl.BlockSpec((ctx_len, HEAD_DIM), lambda h, b: (ctx_row0 + b, 2 * n_heads + h))],
        out_specs=pl.BlockSpec((seq, HEAD_DIM), lambda h, b: (b, h)),
        scratch_shapes=[pltpu.VMEM((len(patterns), kw, qw), F32)],
        compiler_params=_params(("arbitrary", "arbitrary"),
                                8 * seq * HEAD_DIM * 2 + len(patterns) * kw * qw * 4
                                + 2 * len(pairs) * GRID_W * LANES * 4 + 6 * (kw + ctx_len) * qw * 4),
        name="attn_nb",
    )(tz, y, y, y, y, y)


def _rope_tables(seq, pad_rows):
    t = jnp.arange(seq, dtype=jnp.int32)
    row = (t // GRID_W).astype(F32)
    col = (t % GRID_W).astype(F32)
    n_freq = HEAD_DIM // 4
    inv_freq = ROPE_THETA ** (-jnp.arange(n_freq, dtype=F32) / n_freq)
    ang = jnp.concatenate([row[:, None] * inv_freq, col[:, None] * inv_freq], axis=-1)
    cos, sin = jnp.cos(ang), jnp.sin(ang)
    cos2 = jnp.concatenate([cos, cos], axis=-1)
    sin2 = jnp.concatenate([-sin, sin], axis=-1)
    cos2 = jnp.concatenate([cos2, jnp.ones((pad_rows, HEAD_DIM), F32)], axis=0)
    sin2 = jnp.concatenate([sin2, jnp.zeros((pad_rows, HEAD_DIM), F32)], axis=0)
    return cos2, sin2


def _pair_split_index():
    j = jnp.arange(HEAD_DIM, dtype=jnp.int32)
    half = HEAD_DIM // 2
    return jnp.where(j < half, 2 * j, 2 * (j - half) + 1)


def _cast_split_kernel(w_ref, p_ref, o_ref, *, n_split_blocks):
    w = w_ref[...].astype(BF16)

    @pl.when(pl.program_id(1) < n_split_blocks)
    def _():
        for h0 in range(0, w.shape[1], HEAD_DIM):
            o_ref[:, h0:h0 + HEAD_DIM] = jnp.dot(w[:, h0:h0 + HEAD_DIM], p_ref[...],
                                                 preferred_element_type=F32).astype(BF16)

    @pl.when(pl.program_id(1) >= n_split_blocks)
    def _():
        o_ref[...] = w


def _cast_split_pairs(w, n_cols):
    k, n = w.shape
    tk, tn = 1024, 512
    perm = (jnp.arange(HEAD_DIM, dtype=jnp.int32)[:, None] == _pair_split_index()[None, :]).astype(BF16)
    return pl.pallas_call(
        functools.partial(_cast_split_kernel, n_split_blocks=n_cols // tn),
        out_shape=jax.ShapeDtypeStruct((k, n), BF16),
        grid=(k // tk, n // tn),
        in_specs=[pl.BlockSpec((tk, tn), lambda i, j: (i, j)),
                  pl.BlockSpec((HEAD_DIM, HEAD_DIM), lambda i, j: (0, 0))],
        out_specs=pl.BlockSpec((tk, tn), lambda i, j: (i, j)),
        compiler_params=_params(("arbitrary", "arbitrary"), 2 * tk * tn * 4 + 4 * tk * tn * 2 + 2 * tk * tn * 4),
        name="cast_split",
    )(w, perm)


def kernel(x, c, ctx, c_ctx, norm1_g, norm2_g, mod_down, mod_up, mod_b, mlp_w1, mlp_w2, a_w_qkv, a_w_o, a_q_g, a_k_g, b_w_qkv, b_w_o, b_rpb, c_w_qkv, c_w_o, c_lam_q1, c_lam_k1, c_lam_q2, c_lam_k2, c_subln_g, final_g):
    n_batch, seq, d = x.shape
    ctx_len = ctx.shape[1]
    depth = norm1_g.shape[0]
    n_lat = n_batch * seq
    n_all = n_lat + n_batch * ctx_len
    n_mixers = 3
    heads = d // HEAD_DIM

    h = jnp.concatenate([x.reshape(n_lat, d), ctx.reshape(n_batch * ctx_len, d)], axis=0)
    cond = jnp.concatenate([c, c_ctx[None], jnp.zeros((MOD_ROWS - n_batch - 1, d), F32)], axis=0)
    mods = _modulation(cond, mod_down, mod_up, mod_b)
    tm = _row_tile(seq, n_batch * ctx_len)
    rope_tabs = _rope_tables(seq, tm)
    ones_g = jnp.ones((1, HEAD_DIM), F32)

    for i in range(depth):
        with_ctx_out = i < depth - 1
        n_rows = n_all if with_ctx_out else n_lat
        mod = mods[i]
        kind, slot = i % n_mixers, i // n_mixers
        u = _norm_mod(h, norm1_g[i], mod, 0, 1, n_all, seq, n_batch, BF16)
        if kind == 0:
            kv_heads = heads // 4
            n_q, n_kv = heads * HEAD_DIM, kv_heads * HEAD_DIM
            split = _pair_split_index()
            y = _mm_qkv(u, _cast_split_pairs(a_w_qkv[slot], n_q + n_kv), rope_tabs,
                        a_q_g[slot][split][None], a_k_g[slot][split][None], "A", n_q, n_kv, seq, n_batch, tm)
            nq = seq // PIPE_TQ
            o = _pipe_attention(y, d, n_batch, seq, ctx_len, kv_heads, 1, 4, HEAD_DIM,
                                lambda g, it: g * 4 + it // nq, lambda g, it: heads + g,
                                lambda g, it: heads + kv_heads + g, lambda g, it: g * 4 + it // nq,
                                (), (), None, "attn_lat")
            if with_ctx_out:
                o_c = _gqa_attention(y, n_batch, seq, ctx_len, heads, 4, heads, heads + kv_heads, False)
            w_o = a_w_o[slot]
        elif kind == 1:
            y = _mm_qkv(u, b_w_qkv[slot].astype(BF16), rope_tabs, ones_g, ones_g, "B", d, d, seq, n_batch, tm)
            o = _nb_attention(y, b_rpb[slot], n_batch, seq, ctx_len, heads)
            if with_ctx_out:
                o_c = _gqa_attention(y, n_batch, seq, ctx_len, heads, 1, heads, 2 * heads, False)
            w_o = b_w_o[slot]
        else:
            lambda_init = 0.8 - 0.6 * math.exp(-0.3 * i)
            y = _mm_qkv(u, _cast_split_pairs(c_w_qkv[slot], 2 * d), rope_tabs, ones_g, ones_g, "C", d, d,
                        seq, n_batch, tm)
            lam_vecs = jnp.concatenate([c_lam_q1[slot][None], c_lam_k1[slot][None], c_lam_q2[slot][None],
                                        c_lam_k2[slot][None], jnp.zeros((4, HEAD_DIM), F32)], axis=0)
            dh = 2 * HEAD_DIM
            o = _pipe_attention(y, d, n_batch, seq, ctx_len, heads // 2, 2, 1, dh,
                                lambda g, it: 2 * g + it % 2, lambda g, it: heads + 2 * g + it % 2,
                                lambda g, it: heads + g, lambda g, it: g,
                                (lam_vecs, c_subln_g[slot].reshape(1, dh)),
                                (pl.BlockSpec((8, HEAD_DIM), lambda b, g, j: (0, 0)),
                                 pl.BlockSpec((1, dh), lambda b, g, j: (0, 0))),
                                lambda_init, "diff_lat")
            if with_ctx_out:
                o_c = _diff_attention(y, lam_vecs, c_subln_g[slot], lambda_init, n_batch, seq, ctx_len,
                                      heads // 2, False)
            w_o = c_w_o[slot]
        h = _mm_resid(o, o_c if with_ctx_out else None, w_o.astype(BF16), h, mod, 2, n_rows, seq, n_batch, tm)
        v = _norm_mod(h, norm2_g[i], mod, 3, 4, n_rows, seq, n_batch, BF16)
        a = _mm_act(v, mlp_w1[i].astype(BF16), n_rows, True, tm)
        h = _mm_resid(a, None, mlp_w2[i].astype(BF16), h, mod, 5, n_rows, seq, n_batch, tm)

    zeros_mod = jnp.zeros((MOD_ROWS, N_MOD * d), F32)
    out = _norm_mod(h, final_g, zeros_mod, 0, 1, n_lat, seq, n_batch, F32)
    return out.reshape(n_batch, seq, d)
```

```python
import functools
import math

import jax
import jax.numpy as jnp
from jax import lax
from jax.experimental import pallas as pl
from jax.experimental.pallas import tpu as pltpu

F32 = jnp.float32
BF16 = jnp.bfloat16

LANES = 128
V7X_VMEM_BYTES = 64 * 2**20
VMEM_RESERVE_BYTES = 6 * 2**20

HEAD_DIM = 128
GRID_W = 64
WIN_R = 8
WIN_C = 16
MOD_ROWS = 8
N_MOD = 6
ROPE_THETA = 10000.0
EPS = 1e-6
LOG2E = 1.4426950408889634
NEG = -1e30
KEY_CHUNK = 512


def _vmem_limit(block_bytes):
    return int(min(block_bytes + VMEM_RESERVE_BYTES, V7X_VMEM_BYTES - 2**20))


def _params(sem, block_bytes):
    return pltpu.CompilerParams(dimension_semantics=sem, vmem_limit_bytes=_vmem_limit(block_bytes))


def _row_tile(seq, n_ctx_rows, cap=1024):
    tm = cap
    while seq % tm or n_ctx_rows % tm:
        tm //= 2
    return tm


def _mod_row(tile, tm, seq, n_batch):
    start = tile * tm
    return jnp.where(start < n_batch * seq, start // seq, n_batch)


def _mod_down_kernel(c_ref, w_ref, o_ref):
    c = c_ref[...]
    s = c * jax.nn.sigmoid(c)
    o_ref[0] = jnp.dot(s.astype(BF16), w_ref[0].astype(BF16), preferred_element_type=F32)


def _mod_up_kernel(t_ref, w_ref, b_ref, o_ref):
    o_ref[0] = jnp.dot(t_ref[0].astype(BF16), w_ref[0].astype(BF16),
                       preferred_element_type=F32) + b_ref[0]


def _modulation(cond, mod_down, mod_up, mod_b):
    depth, d, rank = mod_down.shape
    n_out = mod_up.shape[2]
    tr = 512
    t = pl.pallas_call(
        _mod_down_kernel,
        out_shape=jax.ShapeDtypeStruct((depth, MOD_ROWS, rank), F32),
        grid=(depth, rank // tr),
        in_specs=[pl.BlockSpec((MOD_ROWS, d), lambda l, j: (0, 0)),
                  pl.BlockSpec((1, d, tr), lambda l, j: (l, 0, j))],
        out_specs=pl.BlockSpec((1, MOD_ROWS, tr), lambda l, j: (l, 0, j)),
        compiler_params=_params(("arbitrary", "arbitrary"), 2 * d * tr * 4 + d * tr * 2 + 4 * MOD_ROWS * d * 4),
        name="mod_down",
    )(cond, mod_down)
    tn = 2048
    return pl.pallas_call(
        _mod_up_kernel,
        out_shape=jax.ShapeDtypeStruct((depth, MOD_ROWS, n_out), F32),
        grid=(depth, n_out // tn),
        in_specs=[pl.BlockSpec((1, MOD_ROWS, rank), lambda l, j: (l, 0, 0)),
                  pl.BlockSpec((1, rank, tn), lambda l, j: (l, 0, j)),
                  pl.BlockSpec((1, 1, tn), lambda l, j: (l, 0, j))],
        out_specs=pl.BlockSpec((1, MOD_ROWS, tn), lambda l, j: (l, 0, j)),
        compiler_params=_params(("arbitrary", "arbitrary"), 2 * rank * tn * 4 + rank * tn * 2),
        name="mod_up",
    )(t, mod_up, mod_b.reshape(depth, 1, n_out))


def _norm_mod_kernel(h_ref, g_ref, sh_ref, sc_ref, o_ref, *, tm, seq, n_batch):
    row = _mod_row(pl.program_id(0), tm, seq, n_batch)
    x = h_ref[...]
    y = x * lax.rsqrt(jnp.mean(x * x, axis=-1, keepdims=True) + EPS)
    y = y * g_ref[...]
    o_ref[...] = (y * (1.0 + sc_ref[pl.ds(row, 1), :]) + sh_ref[pl.ds(row, 1), :]).astype(o_ref.dtype)


def _norm_mod(h, gain, mod, shift_chunk, scale_chunk, n_rows, seq, n_batch, out_dtype, tm):
    d = h.shape[1]
    tm = min(tm, 512)
    kern = functools.partial(_norm_mod_kernel, tm=tm, seq=seq, n_batch=n_batch)
    return pl.pallas_call(
        kern,
        out_shape=jax.ShapeDtypeStruct((n_rows, d), out_dtype),
        grid=(n_rows // tm,),
        in_specs=[pl.BlockSpec((tm, d), lambda i: (i, 0)),
                  pl.BlockSpec((1, d), lambda i: (0, 0)),
                  pl.BlockSpec((MOD_ROWS, d), lambda i: (0, shift_chunk)),
                  pl.BlockSpec((MOD_ROWS, d), lambda i: (0, scale_chunk))],
        out_specs=pl.BlockSpec((tm, d), lambda i: (i, 0)),
        compiler_params=_params(("arbitrary",), 2 * tm * d * 4 + 2 * tm * d * 4 + 3 * tm * d * 4),
        name="norm_mod",
    )(h, gain.reshape(1, d), mod, mod)


def _mm_act_kernel(x_ref, w_ref, o_ref, *, relu2):
    acc = jnp.dot(x_ref[...], w_ref[...], preferred_element_type=F32)
    if relu2:
        acc = jnp.square(jnp.maximum(acc, 0.0))
    o_ref[...] = acc.astype(o_ref.dtype)


def _mm_act(x, w, layer, n_rows, relu2, tm):
    k = x.shape[1]
    n = w.shape[2]
    tn = 1024
    return pl.pallas_call(
        functools.partial(_mm_act_kernel, relu2=relu2),
        out_shape=jax.ShapeDtypeStruct((n_rows, n), BF16),
        grid=(n_rows // tm, n // tn),
        in_specs=[pl.BlockSpec((tm, k), lambda i, j: (i, 0)),
                  pl.BlockSpec((None, k, tn), lambda i, j: (layer, 0, j))],
        out_specs=pl.BlockSpec((tm, tn), lambda i, j: (i, j)),
        compiler_params=_params(("arbitrary", "arbitrary"),
                                2 * tm * k * 2 + 2 * k * tn * 2 + 2 * tm * tn * 2 + 2 * tm * tn * 4),
        name="mm_relu2" if relu2 else "mm_plain",
    )(x, w)


def _mm_resid_kernel(*refs, tm, seq, n_batch, nk, n_lat_tiles):
    if n_lat_tiles is None:
        x_ref, w_ref, h_ref, g_ref, o_ref, acc_ref = refs
    else:
        x_ref, xc_ref, w_ref, h_ref, g_ref, o_ref, acc_ref = refs
    i = pl.program_id(0)
    kk = pl.program_id(2)

    @pl.when(kk == 0)
    def _():
        acc_ref[...] = jnp.zeros_like(acc_ref)

    if n_lat_tiles is None:
        acc_ref[...] += jnp.dot(x_ref[...], w_ref[...], preferred_element_type=F32)
    else:
        @pl.when(i < n_lat_tiles)
        def _():
            acc_ref[...] += jnp.dot(x_ref[...], w_ref[...], preferred_element_type=F32)

        @pl.when(i >= n_lat_tiles)
        def _():
            acc_ref[...] += jnp.dot(xc_ref[...], w_ref[...], preferred_element_type=F32)

    @pl.when(kk == nk - 1)
    def _():
        row = _mod_row(i, tm, seq, n_batch)
        o_ref[...] = h_ref[...] + g_ref[pl.ds(row, 1), :] * acc_ref[...]


def _mm_resid(x, x_ctx, w, layer, h, mod, gate_chunk, n_rows, seq, n_batch, tm):
    k = x.shape[1]
    n = w.shape[2]
    tn = 1024
    tk = min(k, 4096 if x_ctx is None else 2048)
    nk = k // tk
    gate_blk = gate_chunk * (n // tn)
    n_lat_tiles = None if x_ctx is None else x.shape[0] // tm
    kern = functools.partial(_mm_resid_kernel, tm=tm, seq=seq, n_batch=n_batch, nk=nk, n_lat_tiles=n_lat_tiles)
    if x_ctx is None:
        lhs = [x]
        lhs_specs = [pl.BlockSpec((tm, tk), lambda i, j, kk: (i, kk))]
    else:
        lhs = [x, x_ctx]
        lhs_specs = [pl.BlockSpec((tm, tk), lambda i, j, kk: (jnp.minimum(i, n_lat_tiles - 1),
                                                              jnp.where(i < n_lat_tiles, kk, nk - 1))),
                     pl.BlockSpec((tm, tk), lambda i, j, kk: (jnp.maximum(i - n_lat_tiles, 0),
                                                              jnp.where(i < n_lat_tiles, 0, kk)))]
    return pl.pallas_call(
        kern,
        out_shape=jax.ShapeDtypeStruct((n_rows, n), F32),
        grid=(n_rows // tm, n // tn, nk),
        in_specs=lhs_specs + [pl.BlockSpec((None, tk, tn), lambda i, j, kk: (layer, kk, j)),
                              pl.BlockSpec((tm, tn), lambda i, j, kk: (i, j)),
                              pl.BlockSpec((MOD_ROWS, tn), lambda i, j, kk: (0, gate_blk + j))],
        out_specs=pl.BlockSpec((tm, tn), lambda i, j, kk: (i, j)),
        scratch_shapes=[pltpu.VMEM((tm, tn), F32)],
        compiler_params=_params(("arbitrary", "arbitrary", "arbitrary"),
                                2 * len(lhs) * tm * tk * 2 + 2 * tk * tn * 2 + 4 * tm * tn * 4 + 2 * tm * tn * 4),
        name="mm_resid",
    )(*lhs, w, h, mod)


QKV_COL_CHUNK = 512


def _qkv_kernel(x_ref, w_ref, cos_ref, sin_ref, gq_ref, gk_ref, o_ref, *, kind, nq, nk, qscale):
    j = pl.program_id(1)

    def norm(x, g_ref):
        return x * lax.rsqrt(jnp.mean(x * x, axis=-1, keepdims=True) + EPS) * g_ref[...]

    def rope(x):
        return x * cos_ref[...] + pltpu.roll(x, HEAD_DIM // 2, 1) * sin_ref[...]

    chunk = QKV_COL_CHUNK if kind == "C" else o_ref.shape[1]
    full_acc = None if kind == "C" else jnp.dot(x_ref[...], w_ref[...], preferred_element_type=F32)

    def run(fn):
        for c0 in range(0, o_ref.shape[1], chunk):
            if full_acc is None:
                acc = jnp.dot(x_ref[...], w_ref[:, c0:c0 + chunk], preferred_element_type=F32)
            else:
                acc = full_acc
            if fn is None:
                o_ref[:, c0:c0 + chunk] = acc.astype(o_ref.dtype)
                continue
            for h0 in range(0, chunk, HEAD_DIM):
                o_ref[:, c0 + h0:c0 + h0 + HEAD_DIM] = fn(acc[:, h0:h0 + HEAD_DIM]).astype(o_ref.dtype)

    if kind == "A":
        q_fn = lambda x: rope(norm(x, gq_ref)) * qscale
        k_fn = lambda x: rope(norm(x, gk_ref))
    elif kind == "C":
        q_fn = lambda x: rope(x) * qscale
        k_fn = rope
    else:
        q_fn = lambda x: x * qscale
        k_fn = None

    @pl.when(j < nq)
    def _():
        run(q_fn)

    if k_fn is None:
        @pl.when(j >= nq)
        def _():
            run(None)
    else:
        @pl.when((j >= nq) & (j < nq + nk))
        def _():
            run(k_fn)

        @pl.when(j >= nq + nk)
        def _():
            run(None)


def _mm_qkv(x, w, layer, rope_tabs, gq, gk, kind, n_q_cols, n_k_cols, seq, n_batch, tm):
    n_rows, k = x.shape
    n = w.shape[2]
    tn = 1024
    n_lat_tiles = n_batch * seq // tm
    tiles_per_seq = seq // tm
    qscale = HEAD_DIM ** -0.5 * LOG2E

    def rope_map(i, j):
        return (jnp.where(i < n_lat_tiles, i % tiles_per_seq, tiles_per_seq), 0)

    kern = functools.partial(_qkv_kernel, kind=kind, nq=n_q_cols // tn, nk=n_k_cols // tn, qscale=qscale)
    return pl.pallas_call(
        kern,
        out_shape=jax.ShapeDtypeStruct((n_rows, n), BF16),
        grid=(n_rows // tm, n // tn),
        in_specs=[pl.BlockSpec((tm, k), lambda i, j: (i, 0)),
                  pl.BlockSpec((None, k, tn), lambda i, j: (layer, 0, j)),
                  pl.BlockSpec((tm, HEAD_DIM), rope_map),
                  pl.BlockSpec((tm, HEAD_DIM), rope_map),
                  pl.BlockSpec((1, HEAD_DIM), lambda i, j: (0, 0)),
                  pl.BlockSpec((1, HEAD_DIM), lambda i, j: (0, 0))],
        out_specs=pl.BlockSpec((tm, tn), lambda i, j: (i, j)),
        compiler_params=_params(("arbitrary", "arbitrary"),
                                2 * tm * k * 2 + 2 * k * tn * 2 + 2 * tm * tn * 2 + 2 * tm * tn * 4
                                + 4 * tm * HEAD_DIM * 4),
        name="mm_qkv_" + kind,
    )(x, w, *rope_tabs, gq, gk)


def _softmax_step(state, k_blk, v_blk, q):
    m, l, acc = state
    s = lax.dot_general(k_blk, q, (((1,), (1,)), ((), ())), preferred_element_type=F32)
    m_new = jnp.maximum(m, jnp.max(s, axis=0, keepdims=True))
    alpha = jnp.exp2(m - m_new)
    p = jnp.exp2(s - m_new)
    l = alpha * l + jnp.sum(p, axis=0, keepdims=True)
    pv = lax.dot_general(v_blk, p.astype(BF16), (((0,), (0,)), ((), ())), preferred_element_type=F32)
    return m_new, l, alpha * acc + pv


def _softmax_init(tq, dv):
    return (jnp.full((1, tq), NEG, F32), jnp.zeros((1, tq), F32), jnp.zeros((dv, tq), F32))


def _attend(q, kv_refs, dk_slice):
    tq = q.shape[0]
    dv = kv_refs[0][1].shape[1]
    state = _softmax_init(tq, dv)
    for k_ref, v_ref in kv_refs:
        n_keys = k_ref.shape[0]
        step = min(KEY_CHUNK, n_keys)
        for c in range(n_keys // step):
            rows = slice(c * step, (c + 1) * step)
            state = _softmax_step(state, k_ref[rows, dk_slice], v_ref[rows, :], q)
    m, l, acc = state
    return acc / l


def _gqa_kernel(q_ref, *refs):
    o_ref = refs[-1]
    kv = [(refs[i], refs[i + 1]) for i in range(0, len(refs) - 1, 2)]
    o_t = _attend(q_ref[...], kv, slice(None))
    o_ref[...] = o_t.T.astype(o_ref.dtype)


def _gqa_attention(y, n_batch, seq, ctx_len, n_heads, group, k_col0, v_col0, latent):
    n_lat = n_batch * seq
    d_out = n_heads * HEAD_DIM
    if latent:
        tq = 512
        nq = seq // tq
        q_map = lambda b, h, i: (b * nq + i, h)
        n_out = n_lat
    else:
        tq = ctx_len
        nq = 1
        q_map = lambda b, h, i: (n_lat // ctx_len + b, h)
        n_out = n_batch * ctx_len
    ctx_row0 = n_lat // ctx_len
    in_specs = [pl.BlockSpec((tq, HEAD_DIM), q_map)]
    args = [y]
    blk = 2 * tq * HEAD_DIM * 2 * 2
    if latent:
        in_specs += [pl.BlockSpec((seq, HEAD_DIM), lambda b, h, i: (b, k_col0 + h // group)),
                     pl.BlockSpec((seq, HEAD_DIM), lambda b, h, i: (b, v_col0 + h // group))]
        args += [y, y]
        blk += 4 * seq * HEAD_DIM * 2
    in_specs += [pl.BlockSpec((ctx_len, HEAD_DIM), lambda b, h, i: (ctx_row0 + b, k_col0 + h // group)),
                 pl.BlockSpec((ctx_len, HEAD_DIM), lambda b, h, i: (ctx_row0 + b, v_col0 + h // group))]
    args += [y, y]
    if latent:
        out_map = lambda b, h, i: (b * nq + i, h)
    else:
        out_map = lambda b, h, i: (b, h)
    return pl.pallas_call(
        _gqa_kernel,
        out_shape=jax.ShapeDtypeStruct((n_out, d_out), BF16),
        grid=(n_batch, n_heads, nq),
        in_specs=in_specs,
        out_specs=pl.BlockSpec((tq, HEAD_DIM), out_map),
        compiler_params=_params(("arbitrary", "arbitrary", "arbitrary"), blk + 8 * KEY_CHUNK * tq * 4),
        name="attn_lat" if latent else "attn_ctx",
    )(*args)


def _diff_kernel(lam_ref, g_ref, q_ref, *refs, lambda_init):
    o_ref = refs[-1]
    kv = [(refs[i], refs[i + 1]) for i in range(0, len(refs) - 1, 2)]
    lq1, lk1, lq2, lk2 = (lam_ref[r:r + 1, :] for r in range(4))
    lam = (jnp.exp(jnp.sum(lq1 * lk1, axis=-1, keepdims=True))
           - jnp.exp(jnp.sum(lq2 * lk2, axis=-1, keepdims=True)) + lambda_init)
    q = q_ref[...]
    o0 = _attend(q[:, :HEAD_DIM], kv, slice(0, HEAD_DIM))
    o1 = _attend(q[:, HEAD_DIM:], kv, slice(HEAD_DIM, 2 * HEAD_DIM))
    o = (o0 - lam * o1).T
    o = o * lax.rsqrt(jnp.mean(o * o, axis=-1, keepdims=True) + EPS) * g_ref[...]
    o_ref[...] = (o * (1.0 - lambda_init)).astype(o_ref.dtype)


def _diff_attention(y, lam_vecs, subln_g, lambda_init, n_batch, seq, ctx_len, n_heads, latent):
    dh = 2 * HEAD_DIM
    n_lat = n_batch * seq
    ctx_row0 = n_lat // ctx_len
    if latent:
        tq = 512
        nq = seq // tq
        q_map = lambda b, h, i: (b * nq + i, h)
        out_map = q_map
        n_out = n_lat
    else:
        tq = ctx_len
        nq = 1
        q_map = lambda b, h, i: (ctx_row0 + b, h)
        out_map = lambda b, h, i: (b, h)
        n_out = n_batch * ctx_len
    in_specs = [pl.BlockSpec((8, HEAD_DIM), lambda b, h, i: (0, 0)),
                pl.BlockSpec((1, dh), lambda b, h, i: (0, 0)),
                pl.BlockSpec((tq, dh), q_map)]
    args = [lam_vecs, subln_g.reshape(1, dh), y]
    blk = 2 * tq * dh * 2 * 2
    if latent:
        in_specs += [pl.BlockSpec((seq, dh), lambda b, h, i: (b, n_heads + h)),
                     pl.BlockSpec((seq, dh), lambda b, h, i: (b, 2 * n_heads + h))]
        args += [y, y]
        blk += 4 * seq * dh * 2
    in_specs += [pl.BlockSpec((ctx_len, dh), lambda b, h, i: (ctx_row0 + b, n_heads + h)),
                 pl.BlockSpec((ctx_len, dh), lambda b, h, i: (ctx_row0 + b, 2 * n_heads + h))]
    args += [y, y]
    return pl.pallas_call(
        functools.partial(_diff_kernel, lambda_init=lambda_init),
        out_shape=jax.ShapeDtypeStruct((n_out, n_heads * dh), BF16),
        grid=(n_batch, n_heads, nq),
        in_specs=in_specs,
        out_specs=pl.BlockSpec((tq, dh), out_map),
        compiler_params=_params(("arbitrary", "arbitrary", "arbitrary"), blk + 10 * KEY_CHUNK * tq * 4),
        name="diff_lat" if latent else "diff_ctx",
    )(*args)


PIPE_TQ = 512
PIPE_CHUNK = 256


def _pipe_kernel(*refs, diff, lambda_init):
    if diff:
        lam_ref, g_ref, q_ref, k_ref, v_ref, kc_ref, vc_ref, o_ref, sa_ref, sb_ref, m_ref, o0_ref = refs
    else:
        q_ref, k_ref, v_ref, kc_ref, vc_ref, o_ref, sa_ref, sb_ref, m_ref = refs
    j = pl.program_id(2)
    tq = q_ref.shape[0]
    dv = v_ref.shape[1]

    @pl.when(j == 0)
    def _():
        sb_ref[...] = jnp.zeros(sb_ref.shape, F32)
        m_ref[1] = jnp.zeros(m_ref.shape[1:], F32)
        if diff:
            o0_ref[...] = jnp.zeros(o0_ref.shape, F32)

    def body(s_cur, s_prev, cur, prev):
        q = q_ref[...]
        m_new = jnp.full((1, tq), NEG, F32)
        m_old = m_ref[prev][0:1, :]
        l = jnp.zeros((1, tq), F32)
        acc = jnp.zeros((dv, tq), F32)
        blocks = [(k_ref, v_ref, c * PIPE_CHUNK, PIPE_CHUNK) for c in range(k_ref.shape[0] // PIPE_CHUNK)]
        blocks.append((kc_ref, vc_ref, 0, kc_ref.shape[0]))
        off = 0
        for kr, vr, r0, n in blocks:
            s = lax.dot_general(kr[r0:r0 + n, :], q, (((1,), (1,)), ((), ())), preferred_element_type=F32)
            s_cur[off:off + n, :] = s
            m_new = jnp.maximum(m_new, jnp.max(s, axis=0, keepdims=True))
            p = jnp.exp2(s_prev[off:off + n, :] - m_old)
            l = l + jnp.sum(p, axis=0, keepdims=True)
            acc = acc + lax.dot_general(vr[r0:r0 + n, :], p.astype(BF16), (((0,), (0,)), ((), ())),
                                        preferred_element_type=F32)
            off += n
        m_ref[cur] = jnp.broadcast_to(m_new, m_ref.shape[1:])
        o_t = acc / l
        if not diff:
            o_ref[...] = o_t.T.astype(o_ref.dtype)
        elif cur == 1:
            o0_ref[...] = o_t
        else:
            lq1, lk1, lq2, lk2 = (lam_ref[r:r + 1, :] for r in range(4))
            lam = (jnp.exp(jnp.sum(lq1 * lk1, axis=-1, keepdims=True))
                   - jnp.exp(jnp.sum(lq2 * lk2, axis=-1, keepdims=True)) + lambda_init)
            o = (o0_ref[...] - lam * o_t).T
            o = o * lax.rsqrt(jnp.mean(o * o, axis=-1, keepdims=True) + EPS) * g_ref[...]
            o_ref[...] = (o * (1.0 - lambda_init)).astype(o_ref.dtype)

    @pl.when(j % 2 == 0)
    def _():
        body(sa_ref, sb_ref, 0, 1)

    @pl.when(j % 2 == 1)
    def _():
        body(sb_ref, sa_ref, 1, 0)


def _pipe_attention(y, d_out, n_batch, seq, ctx_len, n_outer, items_per_qblock, heads_per_outer, dv,
                    q_col, k_col, v_col, out_col, extra_args, extra_specs, lambda_init, name):
    diff = lambda_init is not None
    n_lat = n_batch * seq
    ctx_row0 = n_lat // ctx_len
    tq = PIPE_TQ
    nq = seq // tq
    n_items = nq * items_per_qblock * heads_per_outer
    n_keys = seq + ctx_len

    def qblock(it):
        return (it // items_per_qblock) % nq

    def cur_item(j):
        return jnp.minimum(j, n_items - 1)

    def prev_item(j):
        return jnp.maximum(j - 1, 0)

    in_specs = list(extra_specs) + [
        pl.BlockSpec((tq, HEAD_DIM), lambda b, g, j: (b * nq + qblock(cur_item(j)), q_col(g, cur_item(j)))),
        pl.BlockSpec((seq, HEAD_DIM), lambda b, g, j: (b, k_col(g, cur_item(j)))),
        pl.BlockSpec((seq, dv), lambda b, g, j: (b, v_col(g, prev_item(j)))),
        pl.BlockSpec((ctx_len, HEAD_DIM), lambda b, g, j: (ctx_row0 + b, k_col(g, cur_item(j)))),
        pl.BlockSpec((ctx_len, dv), lambda b, g, j: (ctx_row0 + b, v_col(g, prev_item(j))))]
    scratch = [pltpu.VMEM((n_keys, tq), F32), pltpu.VMEM((n_keys, tq), F32), pltpu.VMEM((2, 8, tq), F32)]
    if diff:
        scratch.append(pltpu.VMEM((dv, tq), F32))
    out_w = dv if diff else HEAD_DIM
    return pl.pallas_call(
        functools.partial(_pipe_kernel, diff=diff, lambda_init=lambda_init),
        out_shape=jax.ShapeDtypeStruct((n_lat, d_out), BF16),
        grid=(n_batch, n_outer, n_items + 1),
        in_specs=in_specs,
        out_specs=pl.BlockSpec((tq, out_w),
                               lambda b, g, j: (b * nq + qblock(prev_item(j)), out_col(g, prev_item(j)))),
        scratch_shapes=scratch,
        compiler_params=_params(("arbitrary", "arbitrary", "arbitrary"),
                                2 * n_keys * tq * 4 + 4 * n_keys * (HEAD_DIM + dv) * 2 + 8 * tq * dv * 4
                                + 8 * PIPE_CHUNK * tq * 4),
        name=name,
    )(*extra_args, y, y, y, y, y)


NB_Q_ROWS = 8
NB_K_ROWS = 16


def _nb_chunks(rows):
    wr = min(WIN_R, rows)
    chunks, patterns = [], []
    for c in range(rows // NB_Q_ROWS):
        kr0 = min(max(NB_Q_ROWS * c - WIN_R // 2, 0), rows - NB_K_ROWS)
        pat = []
        for a in range(NB_K_ROWS):
            for rq in range(NB_Q_ROWS):
                r = NB_Q_ROWS * c + rq
                r0 = min(max(r - WIN_R // 2, 0), rows - wr)
                kr = kr0 + a
                pat.append(kr - r + (WIN_R - 1) if r0 <= kr < r0 + wr else None)
        pat = tuple(pat)
        if pat not in patterns:
            patterns.append(pat)
        chunks.append((kr0, patterns.index(pat)))
    pairs = []
    for pat in patterns:
        for t in range(0, len(pat), 2):
            if pat[t:t + 2] not in pairs:
                pairs.append(pat[t:t + 2])
    return chunks, patterns, pairs


def _nb_kernel(tz_ref, q_ref, k_ref, v_ref, kc_ref, vc_ref, o_ref, bias_ref, *, chunks, patterns, pairs):
    qw = NB_Q_ROWS * GRID_W
    kw = NB_K_ROWS * GRID_W

    @pl.when(pl.program_id(1) == 0)
    def _():
        for p, pat in enumerate(patterns):
            for a in range(NB_K_ROWS):
                for j in range(NB_Q_ROWS // 2):
                    t = a * NB_Q_ROWS + 2 * j
                    bias_ref[p, a * GRID_W:(a + 1) * GRID_W, j * LANES:(j + 1) * LANES] = (
                        tz_ref[0, pairs.index(pat[t:t + 2])])

    kctx = kc_ref[...]
    vctx = vc_ref[...]

    def scores(c):
        kr0, p = chunks[c]
        q = q_ref[c * qw:(c + 1) * qw, :]
        kb = k_ref[kr0 * GRID_W:kr0 * GRID_W + kw, :]
        dims = (((1,), (1,)), ((), ()))
        s_w = lax.dot_general(kb, q, dims, preferred_element_type=F32) + bias_ref[p]
        s_c = lax.dot_general(kctx, q, dims, preferred_element_type=F32)
        return s_w, s_c

    def finish(c, s_w, s_c):
        kr0 = chunks[c][0]
        vb = v_ref[kr0 * GRID_W:kr0 * GRID_W + kw, :]
        m = jnp.maximum(jnp.max(s_w, axis=0, keepdims=True), jnp.max(s_c, axis=0, keepdims=True))
        p_w = jnp.exp2(s_w - m)
        p_c = jnp.exp2(s_c - m)
        l = jnp.sum(p_w, axis=0, keepdims=True) + jnp.sum(p_c, axis=0, keepdims=True)
        tdims = (((0,), (0,)), ((), ()))
        o_t = (lax.dot_general(vb, p_w.astype(BF16), tdims, preferred_element_type=F32)
               + lax.dot_general(vctx, p_c.astype(BF16), tdims, preferred_element_type=F32))
        o_ref[c * qw:(c + 1) * qw, :] = (o_t / l).T.astype(o_ref.dtype)

    pending = scores(0)
    for c in range(1, len(chunks)):
        upcoming = scores(c)
        finish(c - 1, *pending)
        pending = upcoming
    finish(len(chunks) - 1, *pending)


def _nb_bias_tiles(rpb, pairs):
    kc = jnp.arange(GRID_W, dtype=jnp.int32)[:, None]
    qc = jnp.arange(GRID_W, dtype=jnp.int32)[None, :]
    c0 = jnp.clip(qc - WIN_C // 2, 0, GRID_W - WIN_C)
    in_win = (kc >= c0) & (kc < c0 + WIN_C)
    dc = jnp.clip(kc - qc + (WIN_C - 1), 0, 2 * WIN_C - 2)
    tiles = jnp.where(in_win[None, None], rpb[:, :, dc] * LOG2E, NEG)
    neg = jnp.full((rpb.shape[0], GRID_W, GRID_W), NEG, F32)
    half = lambda dr: neg if dr is None else tiles[:, dr]
    return jnp.stack([jnp.concatenate([half(e), half(o)], axis=-1) for e, o in pairs], axis=1)


def _nb_attention(y, rpb, n_batch, seq, ctx_len, n_heads):
    rows = seq // GRID_W
    assert seq % GRID_W == 0 and rows % NB_Q_ROWS == 0 and rows >= NB_K_ROWS
    chunks, patterns, pairs = _nb_chunks(rows)
    tz = _nb_bias_tiles(rpb, pairs)
    n_lat = n_batch * seq
    ctx_row0 = n_lat // ctx_len
    qw = NB_Q_ROWS * GRID_W
    kw = NB_K_ROWS * GRID_W
    kern = functools.partial(_nb_kernel, chunks=tuple(chunks), patterns=tuple(patterns), pairs=tuple(pairs))
    return pl.pallas_call(
        kern,
        out_shape=jax.ShapeDtypeStruct((n_lat, n_heads * HEAD_DIM), BF16),
        grid=(n_heads, n_batch),
        in_specs=[pl.BlockSpec((1, len(pairs), GRID_W, 2 * GRID_W), lambda h, b: (h, 0, 0, 0)),
                  pl.BlockSpec((seq, HEAD_DIM), lambda h, b: (b, h)),
                  pl.BlockSpec((seq, HEAD_DIM), lambda h, b: (b, n_heads + h)),
                  pl.BlockSpec((seq, HEAD_DIM), lambda h, b: (b, 2 * n_heads + h)),
                  pl.BlockSpec((ctx_len, HEAD_DIM), lambda h, b: (ctx_row0 + b, n_heads + h)),
                  pl.BlockSpec((ctx_len, HEAD_DIM), lambda h, b: (ctx_row0 + b, 2 * n_heads + h))],
        out_specs=pl.BlockSpec((seq, HEAD_DIM), lambda h, b: (b, h)),
        scratch_shapes=[pltpu.VMEM((len(patterns), kw, qw), F32)],
        compiler_params=_params(("arbitrary", "arbitrary"),
                                8 * seq * HEAD_DIM * 2 + len(patterns) * kw * qw * 4
                                + 2 * len(pairs) * GRID_W * LANES * 4 + 6 * (kw + ctx_len) * qw * 4),
        name="attn_nb",
    )(tz, y, y, y, y, y)


def _rope_tables(seq, pad_rows):
    t = jnp.arange(seq, dtype=jnp.int32)
    row = (t // GRID_W).astype(F32)
    col = (t % GRID_W).astype(F32)
    n_freq = HEAD_DIM // 4
    inv_freq = ROPE_THETA ** (-jnp.arange(n_freq, dtype=F32) / n_freq)
    ang = jnp.concatenate([row[:, None] * inv_freq, col[:, None] * inv_freq], axis=-1)
    cos, sin = jnp.cos(ang), jnp.sin(ang)
    cos2 = jnp.concatenate([cos, cos], axis=-1)
    sin2 = jnp.concatenate([-sin, sin], axis=-1)
    cos2 = jnp.concatenate([cos2, jnp.ones((pad_rows, HEAD_DIM), F32)], axis=0)
    sin2 = jnp.concatenate([sin2, jnp.zeros((pad_rows, HEAD_DIM), F32)], axis=0)
    return cos2, sin2


def _pair_split_index():
    j = jnp.arange(HEAD_DIM, dtype=jnp.int32)
    half = HEAD_DIM // 2
    return jnp.where(j < half, 2 * j, 2 * (j - half) + 1)


def _cast_split_kernel(w_ref, p_ref, o_ref, *, n_split_blocks):
    w = w_ref[...].astype(BF16)

    @pl.when(pl.program_id(1) < n_split_blocks)
    def _():
        for h0 in range(0, w.shape[1], HEAD_DIM):
            o_ref[:, h0:h0 + HEAD_DIM] = jnp.dot(w[:, h0:h0 + HEAD_DIM], p_ref[...],
                                                 preferred_element_type=F32).astype(BF16)

    @pl.when(pl.program_id(1) >= n_split_blocks)
    def _():
        o_ref[...] = w


def _cast_split_pairs(w, layer, n_cols):
    _, k, n = w.shape
    tk, tn = 1024, 512
    perm = (jnp.arange(HEAD_DIM, dtype=jnp.int32)[:, None] == _pair_split_index()[None, :]).astype(BF16)
    return pl.pallas_call(
        functools.partial(_cast_split_kernel, n_split_blocks=n_cols // tn),
        out_shape=jax.ShapeDtypeStruct((1, k, n), BF16),
        grid=(k // tk, n // tn),
        in_specs=[pl.BlockSpec((None, tk, tn), lambda i, j: (layer, i, j)),
                  pl.BlockSpec((HEAD_DIM, HEAD_DIM), lambda i, j: (0, 0))],
        out_specs=pl.BlockSpec((None, tk, tn), lambda i, j: (0, i, j)),
        compiler_params=_params(("arbitrary", "arbitrary"), 2 * tk * tn * 4 + 4 * tk * tn * 2 + 2 * tk * tn * 4),
        name="cast_split",
    )(w, perm)


def kernel(x, c, ctx, c_ctx, norm1_g, norm2_g, mod_down, mod_up, mod_b, mlp_w1, mlp_w2, a_w_qkv, a_w_o, a_q_g, a_k_g, b_w_qkv, b_w_o, b_rpb, c_w_qkv, c_w_o, c_lam_q1, c_lam_k1, c_lam_q2, c_lam_k2, c_subln_g, final_g):
    n_batch, seq, d = x.shape
    ctx_len = ctx.shape[1]
    depth = norm1_g.shape[0]
    n_lat = n_batch * seq
    n_all = n_lat + n_batch * ctx_len
    n_mixers = 3
    heads = d // HEAD_DIM

    h = jnp.concatenate([x.reshape(n_lat, d), ctx.reshape(n_batch * ctx_len, d)], axis=0)
    cond = jnp.concatenate([c, c_ctx[None], jnp.zeros((MOD_ROWS - n_batch - 1, d), F32)], axis=0)
    mods = _modulation(cond, mod_down, mod_up, mod_b)
    tm = _row_tile(seq, n_batch * ctx_len)
    rope_tabs = _rope_tables(seq, tm)
    ones_g = jnp.ones((1, HEAD_DIM), F32)
    w1_all, w2_all = mlp_w1.astype(BF16), mlp_w2.astype(BF16)
    wo_all = (a_w_o.astype(BF16), b_w_o.astype(BF16), c_w_o.astype(BF16))

    for i in range(depth):
        with_ctx_out = i < depth - 1
        n_rows = n_all if with_ctx_out else n_lat
        mod = mods[i]
        kind, slot = i % n_mixers, i // n_mixers
        u = _norm_mod(h, norm1_g[i], mod, 0, 1, n_all, seq, n_batch, BF16, tm)
        if kind == 0:
            kv_heads = heads // 4
            n_q, n_kv = heads * HEAD_DIM, kv_heads * HEAD_DIM
            split = _pair_split_index()
            y = _mm_qkv(u, _cast_split_pairs(a_w_qkv, slot, n_q + n_kv), 0, rope_tabs,
                        a_q_g[slot][split][None], a_k_g[slot][split][None], "A", n_q, n_kv, seq, n_batch, tm)
            nq = seq // PIPE_TQ
            o = _pipe_attention(y, d, n_batch, seq, ctx_len, kv_heads, 1, 4, HEAD_DIM,
                                lambda g, it: g * 4 + it // nq, lambda g, it: heads + g,
                                lambda g, it: heads + kv_heads + g, lambda g, it: g * 4 + it // nq,
                                (), (), None, "attn_lat")
            if with_ctx_out:
                o_c = _gqa_attention(y, n_batch, seq, ctx_len, heads, 4, heads, heads + kv_heads, False)
        elif kind == 1:
            y = _mm_qkv(u, b_w_qkv.astype(BF16), slot, rope_tabs, ones_g, ones_g, "B", d, d, seq, n_batch, tm)
            o = _nb_attention(y, b_rpb[slot], n_batch, seq, ctx_len, heads)
            if with_ctx_out:
                o_c = _gqa_attention(y, n_batch, seq, ctx_len, heads, 1, heads, 2 * heads, False)
        else:
            lambda_init = 0.8 - 0.6 * math.exp(-0.3 * i)
            y = _mm_qkv(u, _cast_split_pairs(c_w_qkv, slot, 2 * d), 0, rope_tabs, ones_g, ones_g, "C", d, d,
                        seq, n_batch, tm)
            lam_vecs = jnp.concatenate([c_lam_q1[slot][None], c_lam_k1[slot][None], c_lam_q2[slot][None],
                                        c_lam_k2[slot][None], jnp.zeros((4, HEAD_DIM), F32)], axis=0)
            dh = 2 * HEAD_DIM
            o = _pipe_attention(y, d, n_batch, seq, ctx_len, heads // 2, 2, 1, dh,
                                lambda g, it: 2 * g + it % 2, lambda g, it: heads + 2 * g + it % 2,
                                lambda g, it: heads + g, lambda g, it: g,
                                (lam_vecs, c_subln_g[slot].reshape(1, dh)),
                                (pl.BlockSpec((8, HEAD_DIM), lambda b, g, j: (0, 0)),
                                 pl.BlockSpec((1, dh), lambda b, g, j: (0, 0))),
                                lambda_init, "diff_lat")
            if with_ctx_out:
                o_c = _diff_attention(y, lam_vecs, c_subln_g[slot], lambda_init, n_batch, seq, ctx_len,
                                      heads // 2, False)
        h = _mm_resid(o, o_c if with_ctx_out else None, wo_all[kind], slot, h, mod, 2, n_rows, seq, n_batch, tm)
        v = _norm_mod(h, norm2_g[i], mod, 3, 4, n_rows, seq, n_batch, BF16, tm)
        a = _mm_act(v, w1_all, i, n_rows, True, tm)
        h = _mm_resid(a, None, w2_all, i, h, mod, 5, n_rows, seq, n_batch, tm)

    zeros_mod = jnp.zeros((MOD_ROWS, N_MOD * d), F32)
    out = _norm_mod(h, final_g, zeros_mod, 0, 1, n_lat, seq, n_batch, F32, tm)
    return out.reshape(n_batch, seq, d)
```

```python
import functools
import math

import jax
import jax.numpy as jnp
from jax import lax
from jax.experimental import pallas as pl
from jax.experimental.pallas import tpu as pltpu

F32 = jnp.float32
BF16 = jnp.bfloat16

LANES = 128
V7X_VMEM_BYTES = 64 * 2**20
VMEM_RESERVE_BYTES = 6 * 2**20

HEAD_DIM = 128
GRID_W = 64
WIN_R = 8
WIN_C = 16
MOD_ROWS = 8
N_MOD = 6
ROPE_THETA = 10000.0
EPS = 1e-6
LOG2E = 1.4426950408889634
NEG = -1e30


def _vmem_limit(block_bytes):
    return int(min(block_bytes + VMEM_RESERVE_BYTES, V7X_VMEM_BYTES - 2**20))


def _params(sem, block_bytes):
    return pltpu.CompilerParams(dimension_semantics=sem, vmem_limit_bytes=_vmem_limit(block_bytes))


def _row_tile(seq, n_ctx_rows, cap=1024):
    tm = cap
    while seq % tm or n_ctx_rows % tm:
        tm //= 2
    return tm


def _mod_row(tile, tm, seq, n_batch):
    start = tile * tm
    return jnp.where(start < n_batch * seq, start // seq, n_batch)


def _mod_down_kernel(c_ref, w_ref, o_ref):
    c = c_ref[...]
    s = c * jax.nn.sigmoid(c)
    o_ref[0] = jnp.dot(s.astype(BF16), w_ref[0].astype(BF16), preferred_element_type=F32)


def _mod_up_kernel(t_ref, w_ref, b_ref, o_ref):
    o_ref[0] = jnp.dot(t_ref[0].astype(BF16), w_ref[0].astype(BF16),
                       preferred_element_type=F32) + b_ref[0]


def _modulation(cond, mod_down, mod_up, mod_b):
    depth, d, rank = mod_down.shape
    n_out = mod_up.shape[2]
    tr = 512
    t = pl.pallas_call(
        _mod_down_kernel,
        out_shape=jax.ShapeDtypeStruct((depth, MOD_ROWS, rank), F32),
        grid=(depth, rank // tr),
        in_specs=[pl.BlockSpec((MOD_ROWS, d), lambda l, j: (0, 0)),
                  pl.BlockSpec((1, d, tr), lambda l, j: (l, 0, j))],
        out_specs=pl.BlockSpec((1, MOD_ROWS, tr), lambda l, j: (l, 0, j)),
        compiler_params=_params(("arbitrary", "arbitrary"), 2 * d * tr * 4 + d * tr * 2 + 4 * MOD_ROWS * d * 4),
        name="mod_down",
    )(cond, mod_down)
    tn = 2048
    return pl.pallas_call(
        _mod_up_kernel,
        out_shape=jax.ShapeDtypeStruct((depth, MOD_ROWS, n_out), F32),
        grid=(depth, n_out // tn),
        in_specs=[pl.BlockSpec((1, MOD_ROWS, rank), lambda l, j: (l, 0, 0)),
                  pl.BlockSpec((1, rank, tn), lambda l, j: (l, 0, j)),
                  pl.BlockSpec((1, 1, tn), lambda l, j: (l, 0, j))],
        out_specs=pl.BlockSpec((1, MOD_ROWS, tn), lambda l, j: (l, 0, j)),
        compiler_params=_params(("arbitrary", "arbitrary"), 2 * rank * tn * 4 + rank * tn * 2),
        name="mod_up",
    )(t, mod_up, mod_b.reshape(depth, 1, n_out))


def _norm_mod_kernel(h_ref, g_ref, sh_ref, sc_ref, o_ref, *, tm, seq, n_batch):
    row = _mod_row(pl.program_id(0), tm, seq, n_batch)
    x = h_ref[...]
    y = x * lax.rsqrt(jnp.mean(x * x, axis=-1, keepdims=True) + EPS)
    y = y * g_ref[...]
    o_ref[...] = (y * (1.0 + sc_ref[pl.ds(row, 1), :]) + sh_ref[pl.ds(row, 1), :]).astype(o_ref.dtype)


def _norm_mod(h, gain, mod, shift_chunk, scale_chunk, n_rows, seq, n_batch, out_dtype, tm):
    d = h.shape[1]
    tm = min(tm, 512)
    kern = functools.partial(_norm_mod_kernel, tm=tm, seq=seq, n_batch=n_batch)
    return pl.pallas_call(
        kern,
        out_shape=jax.ShapeDtypeStruct((n_rows, d), out_dtype),
        grid=(n_rows // tm,),
        in_specs=[pl.BlockSpec((tm, d), lambda i: (i, 0)),
                  pl.BlockSpec((1, d), lambda i: (0, 0)),
                  pl.BlockSpec((MOD_ROWS, d), lambda i: (0, shift_chunk)),
                  pl.BlockSpec((MOD_ROWS, d), lambda i: (0, scale_chunk))],
        out_specs=pl.BlockSpec((tm, d), lambda i: (i, 0)),
        compiler_params=_params(("arbitrary",), 2 * tm * d * 4 + 2 * tm * d * 4 + 3 * tm * d * 4),
        name="norm_mod",
    )(h, gain.reshape(1, d), mod, mod)


def _mm_act_kernel(x_ref, w_ref, o_ref, *, relu2):
    acc = jnp.dot(x_ref[...], w_ref[...], preferred_element_type=F32)
    if relu2:
        acc = jnp.square(jnp.maximum(acc, 0.0))
    o_ref[...] = acc.astype(o_ref.dtype)


def _mm_act(x, w, layer, n_rows, relu2, tm):
    k = x.shape[1]
    n = w.shape[2]
    tn = 1024
    return pl.pallas_call(
        functools.partial(_mm_act_kernel, relu2=relu2),
        out_shape=jax.ShapeDtypeStruct((n_rows, n), BF16),
        grid=(n_rows // tm, n // tn),
        in_specs=[pl.BlockSpec((tm, k), lambda i, j: (i, 0)),
                  pl.BlockSpec((None, k, tn), lambda i, j: (layer, 0, j))],
        out_specs=pl.BlockSpec((tm, tn), lambda i, j: (i, j)),
        compiler_params=_params(("arbitrary", "arbitrary"),
                                2 * tm * k * 2 + 2 * k * tn * 2 + 2 * tm * tn * 2 + 2 * tm * tn * 4),
        name="mm_relu2" if relu2 else "mm_plain",
    )(x, w)


def _mm_resid_kernel(*refs, tm, seq, n_batch, nk, n_lat_tiles):
    if n_lat_tiles is None:
        x_ref, w_ref, h_ref, g_ref, o_ref, acc_ref = refs
    else:
        x_ref, xc_ref, w_ref, h_ref, g_ref, o_ref, acc_ref = refs
    i = pl.program_id(0)
    kk = pl.program_id(2)

    @pl.when(kk == 0)
    def _():
        acc_ref[...] = jnp.zeros_like(acc_ref)

    if n_lat_tiles is None:
        acc_ref[...] += jnp.dot(x_ref[...], w_ref[...], preferred_element_type=F32)
    else:
        @pl.when(i < n_lat_tiles)
        def _():
            acc_ref[...] += jnp.dot(x_ref[...], w_ref[...], preferred_element_type=F32)

        @pl.when(i >= n_lat_tiles)
        def _():
            acc_ref[...] += jnp.dot(xc_ref[...], w_ref[...], preferred_element_type=F32)

    @pl.when(kk == nk - 1)
    def _():
        row = _mod_row(i, tm, seq, n_batch)
        o_ref[...] = h_ref[...] + g_ref[pl.ds(row, 1), :] * acc_ref[...]


def _mm_resid(x, x_ctx, w, layer, h, mod, gate_chunk, n_rows, seq, n_batch, tm):
    k = x.shape[1]
    n = w.shape[2]
    tn = 1024
    tk = min(k, 4096 if x_ctx is None else 2048)
    nk = k // tk
    gate_blk = gate_chunk * (n // tn)
    n_lat_tiles = None if x_ctx is None else x.shape[0] // tm
    kern = functools.partial(_mm_resid_kernel, tm=tm, seq=seq, n_batch=n_batch, nk=nk, n_lat_tiles=n_lat_tiles)
    if x_ctx is None:
        lhs = [x]
        lhs_specs = [pl.BlockSpec((tm, tk), lambda i, j, kk: (i, kk))]
    else:
        lhs = [x, x_ctx]
        lhs_specs = [pl.BlockSpec((tm, tk), lambda i, j, kk: (jnp.minimum(i, n_lat_tiles - 1),
                                                              jnp.where(i < n_lat_tiles, kk, nk - 1))),
                     pl.BlockSpec((tm, tk), lambda i, j, kk: (jnp.maximum(i - n_lat_tiles, 0),
                                                              jnp.where(i < n_lat_tiles, 0, kk)))]
    return pl.pallas_call(
        kern,
        out_shape=jax.ShapeDtypeStruct((n_rows, n), F32),
        grid=(n_rows // tm, n // tn, nk),
        in_specs=lhs_specs + [pl.BlockSpec((None, tk, tn), lambda i, j, kk: (layer, kk, j)),
                              pl.BlockSpec((tm, tn), lambda i, j, kk: (i, j)),
                              pl.BlockSpec((MOD_ROWS, tn), lambda i, j, kk: (0, gate_blk + j))],
        out_specs=pl.BlockSpec((tm, tn), lambda i, j, kk: (i, j)),
        scratch_shapes=[pltpu.VMEM((tm, tn), F32)],
        compiler_params=_params(("arbitrary", "arbitrary", "arbitrary"),
                                2 * len(lhs) * tm * tk * 2 + 2 * tk * tn * 2 + 4 * tm * tn * 4 + 2 * tm * tn * 4),
        name="mm_resid",
    )(*lhs, w, h, mod)


QKV_COL_CHUNK = 512


def _qkv_kernel(x_ref, w_ref, cos_ref, sin_ref, gq_ref, gk_ref, o_ref, *, kind, nq, nk, qscale):
    j = pl.program_id(1)

    def norm(x, g_ref):
        return x * lax.rsqrt(jnp.mean(x * x, axis=-1, keepdims=True) + EPS) * g_ref[...]

    def rope(x):
        return x * cos_ref[...] + pltpu.roll(x, HEAD_DIM // 2, 1) * sin_ref[...]

    chunk = o_ref.shape[1] if kind == "A" else QKV_COL_CHUNK
    full_acc = jnp.dot(x_ref[...], w_ref[...], preferred_element_type=F32) if kind == "A" else None

    def run(fn):
        for c0 in range(0, o_ref.shape[1], chunk):
            if full_acc is None:
                acc = jnp.dot(x_ref[...], w_ref[:, c0:c0 + chunk], preferred_element_type=F32)
            else:
                acc = full_acc
            if fn is None:
                o_ref[:, c0:c0 + chunk] = acc.astype(o_ref.dtype)
                continue
            for h0 in range(0, chunk, HEAD_DIM):
                o_ref[:, c0 + h0:c0 + h0 + HEAD_DIM] = fn(acc[:, h0:h0 + HEAD_DIM]).astype(o_ref.dtype)

    if kind == "A":
        q_fn = lambda x: rope(norm(x, gq_ref)) * qscale
        k_fn = lambda x: rope(norm(x, gk_ref))
    elif kind == "C":
        q_fn = lambda x: rope(x) * qscale
        k_fn = rope
    else:
        q_fn = lambda x: x * qscale
        k_fn = None

    @pl.when(j < nq)
    def _():
        run(q_fn)

    if k_fn is None:
        @pl.when(j >= nq)
        def _():
            run(None)
    else:
        @pl.when((j >= nq) & (j < nq + nk))
        def _():
            run(k_fn)

        @pl.when(j >= nq + nk)
        def _():
            run(None)


def _mm_qkv(x, w, layer, rope_tabs, gq, gk, kind, n_q_cols, n_k_cols, seq, n_batch, tm):
    n_rows, k = x.shape
    n = w.shape[2]
    tn = 1024
    n_lat_tiles = n_batch * seq // tm
    tiles_per_seq = seq // tm
    qscale = HEAD_DIM ** -0.5 * LOG2E

    def rope_map(i, j):
        return (jnp.where(i < n_lat_tiles, i % tiles_per_seq, tiles_per_seq), 0)

    kern = functools.partial(_qkv_kernel, kind=kind, nq=n_q_cols // tn, nk=n_k_cols // tn, qscale=qscale)
    return pl.pallas_call(
        kern,
        out_shape=jax.ShapeDtypeStruct((n_rows, n), BF16),
        grid=(n_rows // tm, n // tn),
        in_specs=[pl.BlockSpec((tm, k), lambda i, j: (i, 0)),
                  pl.BlockSpec((None, k, tn), lambda i, j: (layer, 0, j)),
                  pl.BlockSpec((tm, HEAD_DIM), rope_map),
                  pl.BlockSpec((tm, HEAD_DIM), rope_map),
                  pl.BlockSpec((1, HEAD_DIM), lambda i, j: (0, 0)),
                  pl.BlockSpec((1, HEAD_DIM), lambda i, j: (0, 0))],
        out_specs=pl.BlockSpec((tm, tn), lambda i, j: (i, j)),
        compiler_params=_params(("arbitrary", "arbitrary"),
                                2 * tm * k * 2 + 2 * k * tn * 2 + 2 * tm * tn * 2 + 2 * tm * tn * 4
                                + 4 * tm * HEAD_DIM * 4),
        name="mm_qkv_" + kind,
    )(x, w, *rope_tabs, gq, gk)


def _ctx_softmax(q, k, v):
    s = lax.dot_general(k, q, (((1,), (1,)), ((), ())), preferred_element_type=F32)
    p = jnp.exp2(s - jnp.max(s, axis=0, keepdims=True))
    pv = lax.dot_general(v, p.astype(BF16), (((0,), (0,)), ((), ())), preferred_element_type=F32)
    return pv / jnp.sum(p, axis=0, keepdims=True)


def _ctx_gqa_kernel(q_ref, k_ref, v_ref, o_ref):
    o_ref[...] = _ctx_softmax(q_ref[...], k_ref[...], v_ref[...]).T.astype(o_ref.dtype)


def _ctx_gqa_attention(y, n_batch, seq, ctx_len, n_heads, group, k_col0, v_col0):
    ctx_row0 = n_batch * seq // ctx_len
    return pl.pallas_call(
        _ctx_gqa_kernel,
        out_shape=jax.ShapeDtypeStruct((n_batch * ctx_len, n_heads * HEAD_DIM), BF16),
        grid=(n_batch, n_heads),
        in_specs=[pl.BlockSpec((ctx_len, HEAD_DIM), lambda b, h: (ctx_row0 + b, h)),
                  pl.BlockSpec((ctx_len, HEAD_DIM), lambda b, h: (ctx_row0 + b, k_col0 + h // group)),
                  pl.BlockSpec((ctx_len, HEAD_DIM), lambda b, h: (ctx_row0 + b, v_col0 + h // group))],
        out_specs=pl.BlockSpec((ctx_len, HEAD_DIM), lambda b, h: (b, h)),
        compiler_params=_params(("arbitrary", "arbitrary"), 8 * ctx_len * HEAD_DIM * 2 + 4 * ctx_len * ctx_len * 4),
        name="attn_ctx",
    )(y, y, y)


def _diff_lambda(lam_ref, lambda_init):
    lq1, lk1, lq2, lk2 = (lam_ref[r:r + 1, :] for r in range(4))
    return (jnp.exp(jnp.sum(lq1 * lk1, axis=-1, keepdims=True))
            - jnp.exp(jnp.sum(lq2 * lk2, axis=-1, keepdims=True)) + lambda_init)


def _diff_combine(o0_t, o1_t, lam, g_ref, lambda_init):
    o = (o0_t - lam * o1_t).T
    o = o * lax.rsqrt(jnp.mean(o * o, axis=-1, keepdims=True) + EPS) * g_ref[...]
    return o * (1.0 - lambda_init)


def _ctx_diff_kernel(lam_ref, g_ref, q_ref, k_ref, v_ref, o_ref, *, lambda_init):
    q, k, v = q_ref[...], k_ref[...], v_ref[...]
    o0 = _ctx_softmax(q[:, :HEAD_DIM], k[:, :HEAD_DIM], v)
    o1 = _ctx_softmax(q[:, HEAD_DIM:], k[:, HEAD_DIM:], v)
    o_ref[...] = _diff_combine(o0, o1, _diff_lambda(lam_ref, lambda_init), g_ref, lambda_init).astype(o_ref.dtype)


def _ctx_diff_attention(y, lam_vecs, subln_g, lambda_init, n_batch, seq, ctx_len, n_heads):
    dh = 2 * HEAD_DIM
    ctx_row0 = n_batch * seq // ctx_len
    return pl.pallas_call(
        functools.partial(_ctx_diff_kernel, lambda_init=lambda_init),
        out_shape=jax.ShapeDtypeStruct((n_batch * ctx_len, n_heads * dh), BF16),
        grid=(n_batch, n_heads),
        in_specs=[pl.BlockSpec((8, HEAD_DIM), lambda b, h: (0, 0)),
                  pl.BlockSpec((1, dh), lambda b, h: (0, 0)),
                  pl.BlockSpec((ctx_len, dh), lambda b, h: (ctx_row0 + b, h)),
                  pl.BlockSpec((ctx_len, dh), lambda b, h: (ctx_row0 + b, n_heads + h)),
                  pl.BlockSpec((ctx_len, dh), lambda b, h: (ctx_row0 + b, 2 * n_heads + h))],
        out_specs=pl.BlockSpec((ctx_len, dh), lambda b, h: (b, h)),
        compiler_params=_params(("arbitrary", "arbitrary"), 8 * ctx_len * dh * 2 + 6 * ctx_len * ctx_len * 4),
        name="diff_ctx",
    )(lam_vecs, subln_g.reshape(1, dh), y, y, y)


PIPE_TQ = 512
PIPE_CHUNK = 256


def _pipe_kernel(*refs, diff, lambda_init):
    if diff:
        lam_ref, g_ref, q_ref, k_ref, v_ref, kc_ref, vc_ref, o_ref, sa_ref, sb_ref, m_ref, o0_ref = refs
    else:
        q_ref, k_ref, v_ref, kc_ref, vc_ref, o_ref, sa_ref, sb_ref, m_ref = refs
    j = pl.program_id(2)
    tq = q_ref.shape[0]
    dv = v_ref.shape[1]

    @pl.when(j == 0)
    def _():
        sb_ref[...] = jnp.zeros(sb_ref.shape, F32)
        m_ref[1] = jnp.zeros(m_ref.shape[1:], F32)
        if diff:
            o0_ref[...] = jnp.zeros(o0_ref.shape, F32)

    def body(s_cur, s_prev, cur, prev):
        q = q_ref[...]
        m_new = jnp.full((1, tq), NEG, F32)
        m_old = m_ref[prev][0:1, :]
        l = jnp.zeros((1, tq), F32)
        acc = jnp.zeros((dv, tq), F32)
        blocks = [(k_ref, v_ref, c * PIPE_CHUNK, PIPE_CHUNK) for c in range(k_ref.shape[0] // PIPE_CHUNK)]
        blocks.append((kc_ref, vc_ref, 0, kc_ref.shape[0]))
        off = 0
        for kr, vr, r0, n in blocks:
            s = lax.dot_general(kr[r0:r0 + n, :], q, (((1,), (1,)), ((), ())), preferred_element_type=F32)
            s_cur[off:off + n, :] = s
            m_new = jnp.maximum(m_new, jnp.max(s, axis=0, keepdims=True))
            p = jnp.exp2(s_prev[off:off + n, :] - m_old)
            l = l + jnp.sum(p, axis=0, keepdims=True)
            acc = acc + lax.dot_general(vr[r0:r0 + n, :], p.astype(BF16), (((0,), (0,)), ((), ())),
                                        preferred_element_type=F32)
            off += n
        m_ref[cur] = jnp.broadcast_to(m_new, m_ref.shape[1:])
        o_t = acc / l
        if not diff:
            o_ref[...] = o_t.T.astype(o_ref.dtype)
        elif cur == 1:
            o0_ref[...] = o_t
        else:
            lam = _diff_lambda(lam_ref, lambda_init)
            o_ref[...] = _diff_combine(o0_ref[...], o_t, lam, g_ref, lambda_init).astype(o_ref.dtype)

    @pl.when(j % 2 == 0)
    def _():
        body(sa_ref, sb_ref, 0, 1)

    @pl.when(j % 2 == 1)
    def _():
        body(sb_ref, sa_ref, 1, 0)


def _pipe_attention(y, d_out, n_batch, seq, ctx_len, n_outer, items_per_qblock, heads_per_outer, dv,
                    q_col, k_col, v_col, out_col, extra_args, extra_specs, lambda_init, name):
    diff = lambda_init is not None
    n_lat = n_batch * seq
    ctx_row0 = n_lat // ctx_len
    tq = PIPE_TQ
    nq = seq // tq
    n_items = nq * items_per_qblock * heads_per_outer
    n_keys = seq + ctx_len

    def qblock(it):
        return (it // items_per_qblock) % nq

    def cur_item(j):
        return jnp.minimum(j, n_items - 1)

    def prev_item(j):
        return jnp.maximum(j - 1, 0)

    in_specs = list(extra_specs) + [
        pl.BlockSpec((tq, HEAD_DIM), lambda b, g, j: (b * nq + qblock(cur_item(j)), q_col(g, cur_item(j)))),
        pl.BlockSpec((seq, HEAD_DIM), lambda b, g, j: (b, k_col(g, cur_item(j)))),
        pl.BlockSpec((seq, dv), lambda b, g, j: (b, v_col(g, prev_item(j)))),
        pl.BlockSpec((ctx_len, HEAD_DIM), lambda b, g, j: (ctx_row0 + b, k_col(g, cur_item(j)))),
        pl.BlockSpec((ctx_len, dv), lambda b, g, j: (ctx_row0 + b, v_col(g, prev_item(j))))]
    scratch = [pltpu.VMEM((n_keys, tq), F32), pltpu.VMEM((n_keys, tq), F32), pltpu.VMEM((2, 8, tq), F32)]
    if diff:
        scratch.append(pltpu.VMEM((dv, tq), F32))
    out_w = dv if diff else HEAD_DIM
    return pl.pallas_call(
        functools.partial(_pipe_kernel, diff=diff, lambda_init=lambda_init),
        out_shape=jax.ShapeDtypeStruct((n_lat, d_out), BF16),
        grid=(n_batch, n_outer, n_items + 1),
        in_specs=in_specs,
        out_specs=pl.BlockSpec((tq, out_w),
                               lambda b, g, j: (b * nq + qblock(prev_item(j)), out_col(g, prev_item(j)))),
        scratch_shapes=scratch,
        compiler_params=_params(("arbitrary", "arbitrary", "arbitrary"),
                                2 * n_keys * tq * 4 + 4 * n_keys * (HEAD_DIM + dv) * 2 + 8 * tq * dv * 4
                                + 8 * PIPE_CHUNK * tq * 4),
        name=name,
    )(*extra_args, y, y, y, y, y)


NB_Q_ROWS = 8
NB_K_ROWS = 16


def _nb_chunks(rows):
    wr = min(WIN_R, rows)
    chunks, patterns = [], []
    for c in range(rows // NB_Q_ROWS):
        kr0 = min(max(NB_Q_ROWS * c - WIN_R // 2, 0), rows - NB_K_ROWS)
        pat = []
        for a in range(NB_K_ROWS):
            for rq in range(NB_Q_ROWS):
                r = NB_Q_ROWS * c + rq
                r0 = min(max(r - WIN_R // 2, 0), rows - wr)
                kr = kr0 + a
                pat.append(kr - r + (WIN_R - 1) if r0 <= kr < r0 + wr else None)
        pat = tuple(pat)
        if pat not in patterns:
            patterns.append(pat)
        chunks.append((kr0, patterns.index(pat)))
    pairs = []
    for pat in patterns:
        for t in range(0, len(pat), 2):
            if pat[t:t + 2] not in pairs:
                pairs.append(pat[t:t + 2])
    return chunks, patterns, pairs


def _nb_kernel(tz_ref, q_ref, k_ref, v_ref, kc_ref, vc_ref, o_ref, bias_ref, *, chunks, patterns, pairs):
    qw = NB_Q_ROWS * GRID_W
    kw = NB_K_ROWS * GRID_W

    @pl.when(pl.program_id(1) == 0)
    def _():
        for p, pat in enumerate(patterns):
            for a in range(NB_K_ROWS):
                for j in range(NB_Q_ROWS // 2):
                    t = a * NB_Q_ROWS + 2 * j
                    bias_ref[p, a * GRID_W:(a + 1) * GRID_W, j * LANES:(j + 1) * LANES] = (
                        tz_ref[0, pairs.index(pat[t:t + 2])])

    kctx = kc_ref[...]
    vctx = vc_ref[...]

    def scores(c):
        kr0, p = chunks[c]
        q = q_ref[c * qw:(c + 1) * qw, :]
        kb = k_ref[kr0 * GRID_W:kr0 * GRID_W + kw, :]
        dims = (((1,), (1,)), ((), ()))
        s_w = lax.dot_general(kb, q, dims, preferred_element_type=F32) + bias_ref[p]
        s_c = lax.dot_general(kctx, q, dims, preferred_element_type=F32)
        return s_w, s_c

    def finish(c, s_w, s_c):
        kr0 = chunks[c][0]
        vb = v_ref[kr0 * GRID_W:kr0 * GRID_W + kw, :]
        m = jnp.maximum(jnp.max(s_w, axis=0, keepdims=True), jnp.max(s_c, axis=0, keepdims=True))
        p_w = jnp.exp2(s_w - m)
        p_c = jnp.exp2(s_c - m)
        l = jnp.sum(p_w, axis=0, keepdims=True) + jnp.sum(p_c, axis=0, keepdims=True)
        tdims = (((0,), (0,)), ((), ()))
        o_t = (lax.dot_general(vb, p_w.astype(BF16), tdims, preferred_element_type=F32)
               + lax.dot_general(vctx, p_c.astype(BF16), tdims, preferred_element_type=F32))
        o_ref[c * qw:(c + 1) * qw, :] = (o_t / l).T.astype(o_ref.dtype)

    pending = scores(0)
    for c in range(1, len(chunks)):
        upcoming = scores(c)
        finish(c - 1, *pending)
        pending = upcoming
    finish(len(chunks) - 1, *pending)


def _nb_bias_tiles(rpb, pairs):
    kc = jnp.arange(GRID_W, dtype=jnp.int32)[:, None]
    qc = jnp.arange(GRID_W, dtype=jnp.int32)[None, :]
    c0 = jnp.clip(qc - WIN_C // 2, 0, GRID_W - WIN_C)
    in_win = (kc >= c0) & (kc < c0 + WIN_C)
    dc = jnp.clip(kc - qc + (WIN_C - 1), 0, 2 * WIN_C - 2)
    tiles = jnp.where(in_win[None, None], rpb[:, :, dc] * LOG2E, NEG)
    neg = jnp.full((rpb.shape[0], GRID_W, GRID_W), NEG, F32)
    half = lambda dr: neg if dr is None else tiles[:, dr]
    return jnp.stack([jnp.concatenate([half(e), half(o)], axis=-1) for e, o in pairs], axis=1)


def _nb_attention(y, rpb, n_batch, seq, ctx_len, n_heads):
    rows = seq // GRID_W
    assert seq % GRID_W == 0 and rows % NB_Q_ROWS == 0 and rows >= NB_K_ROWS
    chunks, patterns, pairs = _nb_chunks(rows)
    tz = _nb_bias_tiles(rpb, pairs)
    n_lat = n_batch * seq
    ctx_row0 = n_lat // ctx_len
    qw = NB_Q_ROWS * GRID_W
    kw = NB_K_ROWS * GRID_W
    kern = functools.partial(_nb_kernel, chunks=tuple(chunks), patterns=tuple(patterns), pairs=tuple(pairs))
    return pl.pallas_call(
        kern,
        out_shape=jax.ShapeDtypeStruct((n_lat, n_heads * HEAD_DIM), BF16),
        grid=(n_heads, n_batch),
        in_specs=[pl.BlockSpec((1, len(pairs), GRID_W, 2 * GRID_W), lambda h, b: (h, 0, 0, 0)),
                  pl.BlockSpec((seq, HEAD_DIM), lambda h, b: (b, h)),
                  pl.BlockSpec((seq, HEAD_DIM), lambda h, b: (b, n_heads + h)),
                  pl.BlockSpec((seq, HEAD_DIM), lambda h, b: (b, 2 * n_heads + h)),
                  pl.BlockSpec((ctx_len, HEAD_DIM), lambda h, b: (ctx_row0 + b, n_heads + h)),
                  pl.BlockSpec((ctx_len, HEAD_DIM), lambda h, b: (ctx_row0 + b, 2 * n_heads + h))],
        out_specs=pl.BlockSpec((seq, HEAD_DIM), lambda h, b: (b, h)),
        scratch_shapes=[pltpu.VMEM((len(patterns), kw, qw), F32)],
        compiler_params=_params(("arbitrary", "arbitrary"),
                                8 * seq * HEAD_DIM * 2 + len(patterns) * kw * qw * 4
                                + 2 * len(pairs) * GRID_W * LANES * 4 + 6 * (kw + ctx_len) * qw * 4),
        name="attn_nb",
    )(tz, y, y, y, y, y)


def _rope_tables(seq, pad_rows):
    t = jnp.arange(seq, dtype=jnp.int32)
    row = (t // GRID_W).astype(F32)
    col = (t % GRID_W).astype(F32)
    n_freq = HEAD_DIM // 4
    inv_freq = ROPE_THETA ** (-jnp.arange(n_freq, dtype=F32) / n_freq)
    ang = jnp.concatenate([row[:, None] * inv_freq, col[:, None] * inv_freq], axis=-1)
    cos, sin = jnp.cos(ang), jnp.sin(ang)
    cos2 = jnp.concatenate([cos, cos], axis=-1)
    sin2 = jnp.concatenate([-sin, sin], axis=-1)
    cos2 = jnp.concatenate([cos2, jnp.ones((pad_rows, HEAD_DIM), F32)], axis=0)
    sin2 = jnp.concatenate([sin2, jnp.zeros((pad_rows, HEAD_DIM), F32)], axis=0)
    return cos2, sin2


def _pair_split_index():
    j = jnp.arange(HEAD_DIM, dtype=jnp.int32)
    half = HEAD_DIM // 2
    return jnp.where(j < half, 2 * j, 2 * (j - half) + 1)


def _cast_split_kernel(w_ref, p_ref, o_ref, *, n_split_blocks):
    w = w_ref[...].astype(BF16)

    @pl.when(pl.program_id(1) < n_split_blocks)
    def _():
        for h0 in range(0, w.shape[1], HEAD_DIM):
            o_ref[:, h0:h0 + HEAD_DIM] = jnp.dot(w[:, h0:h0 + HEAD_DIM], p_ref[...],
                                                 preferred_element_type=F32).astype(BF16)

    @pl.when(pl.program_id(1) >= n_split_blocks)
    def _():
        o_ref[...] = w


def _cast_split_pairs(w, layer, n_cols):
    _, k, n = w.shape
    tk, tn = 1024, 512
    perm = (jnp.arange(HEAD_DIM, dtype=jnp.int32)[:, None] == _pair_split_index()[None, :]).astype(BF16)
    return pl.pallas_call(
        functools.partial(_cast_split_kernel, n_split_blocks=n_cols // tn),
        out_shape=jax.ShapeDtypeStruct((1, k, n), BF16),
        grid=(k // tk, n // tn),
        in_specs=[pl.BlockSpec((None, tk, tn), lambda i, j: (layer, i, j)),
                  pl.BlockSpec((HEAD_DIM, HEAD_DIM), lambda i, j: (0, 0))],
        out_specs=pl.BlockSpec((None, tk, tn), lambda i, j: (0, i, j)),
        compiler_params=_params(("arbitrary", "arbitrary"), 2 * tk * tn * 4 + 4 * tk * tn * 2 + 2 * tk * tn * 4),
        name="cast_split",
    )(w, perm)


def kernel(x, c, ctx, c_ctx, norm1_g, norm2_g, mod_down, mod_up, mod_b, mlp_w1, mlp_w2, a_w_qkv, a_w_o, a_q_g, a_k_g, b_w_qkv, b_w_o, b_rpb, c_w_qkv, c_w_o, c_lam_q1, c_lam_k1, c_lam_q2, c_lam_k2, c_subln_g, final_g):
    n_batch, seq, d = x.shape
    ctx_len = ctx.shape[1]
    depth = norm1_g.shape[0]
    n_lat = n_batch * seq
    n_all = n_lat + n_batch * ctx_len
    n_mixers = 3
    heads = d // HEAD_DIM

    h = jnp.concatenate([x.reshape(n_lat, d), ctx.reshape(n_batch * ctx_len, d)], axis=0)
    cond = jnp.concatenate([c, c_ctx[None], jnp.zeros((MOD_ROWS - n_batch - 1, d), F32)], axis=0)
    mods = _modulation(cond, mod_down, mod_up, mod_b)
    tm = _row_tile(seq, n_batch * ctx_len)
    rope_tabs = _rope_tables(seq, tm)
    ones_g = jnp.ones((1, HEAD_DIM), F32)
    w1_all, w2_all = mlp_w1.astype(BF16), mlp_w2.astype(BF16)
    wo_all = (a_w_o.astype(BF16), b_w_o.astype(BF16), c_w_o.astype(BF16))

    for i in range(depth):
        with_ctx_out = i < depth - 1
        n_rows = n_all if with_ctx_out else n_lat
        mod = mods[i]
        kind, slot = i % n_mixers, i // n_mixers
        u = _norm_mod(h, norm1_g[i], mod, 0, 1, n_all, seq, n_batch, BF16, tm)
        if kind == 0:
            kv_heads = heads // 4
            n_q, n_kv = heads * HEAD_DIM, kv_heads * HEAD_DIM
            split = _pair_split_index()
            y = _mm_qkv(u, _cast_split_pairs(a_w_qkv, slot, n_q + n_kv), 0, rope_tabs,
                        a_q_g[slot][split][None], a_k_g[slot][split][None], "A", n_q, n_kv, seq, n_batch, tm)
            nq = seq // PIPE_TQ
            o = _pipe_attention(y, d, n_batch, seq, ctx_len, kv_heads, 1, 4, HEAD_DIM,
                                lambda g, it: g * 4 + it // nq, lambda g, it: heads + g,
                                lambda g, it: heads + kv_heads + g, lambda g, it: g * 4 + it // nq,
                                (), (), None, "attn_lat")
            if with_ctx_out:
                o_c = _ctx_gqa_attention(y, n_batch, seq, ctx_len, heads, 4, heads, heads + kv_heads)
        elif kind == 1:
            y = _mm_qkv(u, b_w_qkv.astype(BF16), slot, rope_tabs, ones_g, ones_g, "B", d, d, seq, n_batch, tm)
            o = _nb_attention(y, b_rpb[slot], n_batch, seq, ctx_len, heads)
            if with_ctx_out:
                o_c = _ctx_gqa_attention(y, n_batch, seq, ctx_len, heads, 1, heads, 2 * heads)
        else:
            lambda_init = 0.8 - 0.6 * math.exp(-0.3 * i)
            y = _mm_qkv(u, _cast_split_pairs(c_w_qkv, slot, 2 * d), 0, rope_tabs, ones_g, ones_g, "C", d, d,
                        seq, n_batch, tm)
            lam_vecs = jnp.concatenate([c_lam_q1[slot][None], c_lam_k1[slot][None], c_lam_q2[slot][None],
                                        c_lam_k2[slot][None], jnp.zeros((4, HEAD_DIM), F32)], axis=0)
            dh = 2 * HEAD_DIM
            o = _pipe_attention(y, d, n_batch, seq, ctx_len, heads // 2, 2, 1, dh,
                                lambda g, it: 2 * g + it % 2, lambda g, it: heads + 2 * g + it % 2,
                                lambda g, it: heads + g, lambda g, it: g,
                                (lam_vecs, c_subln_g[slot].reshape(1, dh)),
                                (pl.BlockSpec((8, HEAD_DIM), lambda b, g, j: (0, 0)),
                                 pl.BlockSpec((1, dh), lambda b, g, j: (0, 0))),
                                lambda_init, "diff_lat")
            if with_ctx_out:
                o_c = _ctx_diff_attention(y, lam_vecs, c_subln_g[slot], lambda_init, n_batch, seq, ctx_len,
                                          heads // 2)
        h = _mm_resid(o, o_c if with_ctx_out else None, wo_all[kind], slot, h, mod, 2, n_rows, seq, n_batch, tm)
        v = _norm_mod(h, norm2_g[i], mod, 3, 4, n_rows, seq, n_batch, BF16, tm)
        a = _mm_act(v, w1_all, i, n_rows, True, tm)
        h = _mm_resid(a, None, w2_all, i, h, mod, 5, n_rows, seq, n_batch, tm)

    zeros_mod = jnp.zeros((MOD_ROWS, N_MOD * d), F32)
    out = _norm_mod(h, final_g, zeros_mod, 0, 1, n_lat, seq, n_batch, F32, tm)
    return out.reshape(n_batch, seq, d)
```

```python
import functools
import math

import jax
import jax.numpy as jnp
from jax import lax
from jax.experimental import pallas as pl
from jax.experimental.pallas import tpu as pltpu

F32 = jnp.float32
BF16 = jnp.bfloat16

LANES = 128
V7X_VMEM_BYTES = 64 * 2**20
VMEM_RESERVE_BYTES = 6 * 2**20

HEAD_DIM = 128
GRID_W = 64
WIN_R = 8
WIN_C = 16
MOD_ROWS = 8
N_MOD = 6
ROPE_THETA = 10000.0
EPS = 1e-6
LOG2E = 1.4426950408889634
NEG = -1e30


def _vmem_limit(block_bytes):
    return int(min(block_bytes + VMEM_RESERVE_BYTES, V7X_VMEM_BYTES - 2**20))


def _params(sem, block_bytes):
    return pltpu.CompilerParams(dimension_semantics=sem, vmem_limit_bytes=_vmem_limit(block_bytes))


def _row_tile(seq, n_ctx_rows, cap=1024):
    tm = cap
    while seq % tm or n_ctx_rows % tm:
        tm //= 2
    return tm


def _mod_row(tile, tm, seq, n_batch):
    start = tile * tm
    return jnp.where(start < n_batch * seq, start // seq, n_batch)


def _mod_down_kernel(c_ref, w_ref, o_ref):
    c = c_ref[...]
    s = c * jax.nn.sigmoid(c)
    o_ref[0] = jnp.dot(s.astype(BF16), w_ref[0].astype(BF16), preferred_element_type=F32)


def _mod_up_kernel(t_ref, w_ref, b_ref, o_ref):
    o_ref[0] = jnp.dot(t_ref[0].astype(BF16), w_ref[0].astype(BF16),
                       preferred_element_type=F32) + b_ref[0]


def _modulation(cond, mod_down, mod_up, mod_b):
    depth, d, rank = mod_down.shape
    n_out = mod_up.shape[2]
    tr = 512
    t = pl.pallas_call(
        _mod_down_kernel,
        out_shape=jax.ShapeDtypeStruct((depth, MOD_ROWS, rank), F32),
        grid=(depth, rank // tr),
        in_specs=[pl.BlockSpec((MOD_ROWS, d), lambda l, j: (0, 0)),
                  pl.BlockSpec((1, d, tr), lambda l, j: (l, 0, j))],
        out_specs=pl.BlockSpec((1, MOD_ROWS, tr), lambda l, j: (l, 0, j)),
        compiler_params=_params(("arbitrary", "arbitrary"), 2 * d * tr * 4 + d * tr * 2 + 4 * MOD_ROWS * d * 4),
        name="mod_down",
    )(cond, mod_down)
    tn = 2048
    return pl.pallas_call(
        _mod_up_kernel,
        out_shape=jax.ShapeDtypeStruct((depth, MOD_ROWS, n_out), F32),
        grid=(depth, n_out // tn),
        in_specs=[pl.BlockSpec((1, MOD_ROWS, rank), lambda l, j: (l, 0, 0)),
                  pl.BlockSpec((1, rank, tn), lambda l, j: (l, 0, j)),
                  pl.BlockSpec((1, 1, tn), lambda l, j: (l, 0, j))],
        out_specs=pl.BlockSpec((1, MOD_ROWS, tn), lambda l, j: (l, 0, j)),
        compiler_params=_params(("arbitrary", "arbitrary"), 2 * rank * tn * 4 + rank * tn * 2),
        name="mod_up",
    )(t, mod_up, mod_b.reshape(depth, 1, n_out))


def _norm_mod_kernel(h_ref, g_ref, sh_ref, sc_ref, o_ref, *, tm, seq, n_batch):
    row = _mod_row(pl.program_id(0), tm, seq, n_batch)
    x = h_ref[...]
    y = x * lax.rsqrt(jnp.mean(x * x, axis=-1, keepdims=True) + EPS)
    y = y * g_ref[...]
    o_ref[...] = (y * (1.0 + sc_ref[pl.ds(row, 1), :]) + sh_ref[pl.ds(row, 1), :]).astype(o_ref.dtype)


def _norm_mod(h, gain, mod, shift_chunk, scale_chunk, n_rows, seq, n_batch, out_dtype, tm):
    d = h.shape[1]
    tm = min(tm, 512)
    kern = functools.partial(_norm_mod_kernel, tm=tm, seq=seq, n_batch=n_batch)
    return pl.pallas_call(
        kern,
        out_shape=jax.ShapeDtypeStruct((n_rows, d), out_dtype),
        grid=(n_rows // tm,),
        in_specs=[pl.BlockSpec((tm, d), lambda i: (i, 0)),
                  pl.BlockSpec((1, d), lambda i: (0, 0)),
                  pl.BlockSpec((MOD_ROWS, d), lambda i: (0, shift_chunk)),
                  pl.BlockSpec((MOD_ROWS, d), lambda i: (0, scale_chunk))],
        out_specs=pl.BlockSpec((tm, d), lambda i: (i, 0)),
        compiler_params=_params(("arbitrary",), 2 * tm * d * 4 + 2 * tm * d * 4 + 3 * tm * d * 4),
        name="norm_mod",
    )(h, gain.reshape(1, d), mod, mod)


def _mm_act_kernel(x_ref, w_ref, o_ref, *, relu2):
    acc = jnp.dot(x_ref[...], w_ref[...], preferred_element_type=F32)
    if relu2:
        acc = jnp.square(jnp.maximum(acc, 0.0))
    o_ref[...] = acc.astype(o_ref.dtype)


def _mm_act(x, w, layer, n_rows, relu2, tm):
    k = x.shape[1]
    n = w.shape[2]
    tn = 1024
    return pl.pallas_call(
        functools.partial(_mm_act_kernel, relu2=relu2),
        out_shape=jax.ShapeDtypeStruct((n_rows, n), BF16),
        grid=(n_rows // tm, n // tn),
        in_specs=[pl.BlockSpec((tm, k), lambda i, j: (i, 0)),
                  pl.BlockSpec((None, k, tn), lambda i, j: (layer, 0, j))],
        out_specs=pl.BlockSpec((tm, tn), lambda i, j: (i, j)),
        compiler_params=_params(("arbitrary", "arbitrary"),
                                2 * tm * k * 2 + 2 * k * tn * 2 + 2 * tm * tn * 2 + 2 * tm * tn * 4),
        name="mm_relu2" if relu2 else "mm_plain",
    )(x, w)


def _mm_resid_kernel(*refs, tm, seq, n_batch, nk, n_lat_tiles):
    i = pl.program_id(0)
    if nk == 1:
        lhs_refs, (w_ref, h_ref, g_ref, o_ref) = refs[:-4], refs[-4:]

        def direct(x_ref):
            gate = g_ref[pl.ds(_mod_row(i, tm, seq, n_batch), 1), :]
            o_ref[...] = h_ref[...] + gate * jnp.dot(x_ref[...], w_ref[...], preferred_element_type=F32)

        if n_lat_tiles is None:
            direct(lhs_refs[0])
        else:
            pl.when(i < n_lat_tiles)(lambda: direct(lhs_refs[0]))
            pl.when(i >= n_lat_tiles)(lambda: direct(lhs_refs[1]))
        return
    if n_lat_tiles is None:
        x_ref, w_ref, h_ref, g_ref, o_ref, acc_ref = refs
    else:
        x_ref, xc_ref, w_ref, h_ref, g_ref, o_ref, acc_ref = refs
    kk = pl.program_id(2)

    @pl.when(kk == 0)
    def _():
        acc_ref[...] = jnp.zeros_like(acc_ref)

    if n_lat_tiles is None:
        acc_ref[...] += jnp.dot(x_ref[...], w_ref[...], preferred_element_type=F32)
    else:
        @pl.when(i < n_lat_tiles)
        def _():
            acc_ref[...] += jnp.dot(x_ref[...], w_ref[...], preferred_element_type=F32)

        @pl.when(i >= n_lat_tiles)
        def _():
            acc_ref[...] += jnp.dot(xc_ref[...], w_ref[...], preferred_element_type=F32)

    @pl.when(kk == nk - 1)
    def _():
        row = _mod_row(i, tm, seq, n_batch)
        o_ref[...] = h_ref[...] + g_ref[pl.ds(row, 1), :] * acc_ref[...]


def _mm_resid(x, x_ctx, w, layer, h, mod, gate_chunk, n_rows, seq, n_batch, tm):
    k = x.shape[1]
    n = w.shape[2]
    tk = min(k, 4096)
    nk = k // tk
    tn = 512 if nk == 1 else 1024
    gate_blk = gate_chunk * (n // tn)
    n_lat_tiles = None if x_ctx is None else x.shape[0] // tm
    kern = functools.partial(_mm_resid_kernel, tm=tm, seq=seq, n_batch=n_batch, nk=nk, n_lat_tiles=n_lat_tiles)
    if x_ctx is None:
        lhs = [x]
        lhs_specs = [pl.BlockSpec((tm, tk), lambda i, j, kk: (i, kk))]
    else:
        lhs = [x, x_ctx]
        lhs_specs = [pl.BlockSpec((tm, tk), lambda i, j, kk: (jnp.minimum(i, n_lat_tiles - 1),
                                                              jnp.where(i < n_lat_tiles, kk, nk - 1))),
                     pl.BlockSpec((tm, tk), lambda i, j, kk: (jnp.maximum(i - n_lat_tiles, 0),
                                                              jnp.where(i < n_lat_tiles, 0, kk)))]
    return pl.pallas_call(
        kern,
        out_shape=jax.ShapeDtypeStruct((n_rows, n), F32),
        grid=(n_rows // tm, n // tn, nk),
        in_specs=lhs_specs + [pl.BlockSpec((None, tk, tn), lambda i, j, kk: (layer, kk, j)),
                              pl.BlockSpec((tm, tn), lambda i, j, kk: (i, j)),
                              pl.BlockSpec((MOD_ROWS, tn), lambda i, j, kk: (0, gate_blk + j))],
        out_specs=pl.BlockSpec((tm, tn), lambda i, j, kk: (i, j)),
        scratch_shapes=[] if nk == 1 else [pltpu.VMEM((tm, tn), F32)],
        compiler_params=_params(("arbitrary", "arbitrary", "arbitrary"),
                                2 * len(lhs) * tm * tk * 2 + 2 * tk * tn * 2 + 4 * tm * tn * 4 + 2 * tm * tn * 4),
        name="mm_resid",
    )(*lhs, w, h, mod)


QKV_COL_CHUNK = 512


def _qkv_kernel(x_ref, w_ref, cos_ref, sin_ref, gq_ref, gk_ref, o_ref, *, kind, nq, nk, qscale):
    j = pl.program_id(1)

    def norm(x, g_ref):
        return x * lax.rsqrt(jnp.mean(x * x, axis=-1, keepdims=True) + EPS) * g_ref[...]

    def rope(x):
        return x * cos_ref[...] + pltpu.roll(x, HEAD_DIM // 2, 1) * sin_ref[...]

    chunk = o_ref.shape[1] if kind == "A" else QKV_COL_CHUNK
    full_acc = jnp.dot(x_ref[...], w_ref[...], preferred_element_type=F32) if kind == "A" else None

    def run(fn):
        for c0 in range(0, o_ref.shape[1], chunk):
            if full_acc is None:
                acc = jnp.dot(x_ref[...], w_ref[:, c0:c0 + chunk], preferred_element_type=F32)
            else:
                acc = full_acc
            if fn is None:
                o_ref[:, c0:c0 + chunk] = acc.astype(o_ref.dtype)
                continue
            for h0 in range(0, chunk, HEAD_DIM):
                o_ref[:, c0 + h0:c0 + h0 + HEAD_DIM] = fn(acc[:, h0:h0 + HEAD_DIM]).astype(o_ref.dtype)

    if kind == "A":
        q_fn = lambda x: rope(norm(x, gq_ref)) * qscale
        k_fn = lambda x: rope(norm(x, gk_ref))
    elif kind == "C":
        q_fn = lambda x: rope(x) * qscale
        k_fn = rope
    else:
        q_fn = lambda x: x * qscale
        k_fn = None

    @pl.when(j < nq)
    def _():
        run(q_fn)

    if k_fn is None:
        @pl.when(j >= nq)
        def _():
            run(None)
    else:
        @pl.when((j >= nq) & (j < nq + nk))
        def _():
            run(k_fn)

        @pl.when(j >= nq + nk)
        def _():
            run(None)


def _mm_qkv(x, w, layer, rope_tabs, gq, gk, kind, n_q_cols, n_k_cols, seq, n_batch, tm):
    n_rows, k = x.shape
    n = w.shape[2]
    tn = 1024
    n_lat_tiles = n_batch * seq // tm
    tiles_per_seq = seq // tm
    qscale = HEAD_DIM ** -0.5 * LOG2E

    def rope_map(i, j):
        return (jnp.where(i < n_lat_tiles, i % tiles_per_seq, tiles_per_seq), 0)

    kern = functools.partial(_qkv_kernel, kind=kind, nq=n_q_cols // tn, nk=n_k_cols // tn, qscale=qscale)
    return pl.pallas_call(
        kern,
        out_shape=jax.ShapeDtypeStruct((n_rows, n), BF16),
        grid=(n_rows // tm, n // tn),
        in_specs=[pl.BlockSpec((tm, k), lambda i, j: (i, 0)),
                  pl.BlockSpec((None, k, tn), lambda i, j: (layer, 0, j)),
                  pl.BlockSpec((tm, HEAD_DIM), rope_map),
                  pl.BlockSpec((tm, HEAD_DIM), rope_map),
                  pl.BlockSpec((1, HEAD_DIM), lambda i, j: (0, 0)),
                  pl.BlockSpec((1, HEAD_DIM), lambda i, j: (0, 0))],
        out_specs=pl.BlockSpec((tm, tn), lambda i, j: (i, j)),
        compiler_params=_params(("arbitrary", "arbitrary"),
                                2 * tm * k * 2 + 2 * k * tn * 2 + 2 * tm * tn * 2 + 2 * tm * tn * 4
                                + 4 * tm * HEAD_DIM * 4),
        name="mm_qkv_" + kind,
    )(x, w, *rope_tabs, gq, gk)


def _ctx_softmax(q, k, v):
    s = lax.dot_general(k, q, (((1,), (1,)), ((), ())), preferred_element_type=F32)
    p = jnp.exp2(s - jnp.max(s, axis=0, keepdims=True))
    pv = lax.dot_general(v, p.astype(BF16), (((0,), (0,)), ((), ())), preferred_element_type=F32)
    return pv / jnp.sum(p, axis=0, keepdims=True)


def _ctx_gqa_kernel(q_ref, k_ref, v_ref, o_ref):
    o_ref[...] = _ctx_softmax(q_ref[...], k_ref[...], v_ref[...]).T.astype(o_ref.dtype)


def _ctx_gqa_attention(y, n_batch, seq, ctx_len, n_heads, group, k_col0, v_col0):
    ctx_row0 = n_batch * seq // ctx_len
    return pl.pallas_call(
        _ctx_gqa_kernel,
        out_shape=jax.ShapeDtypeStruct((n_batch * ctx_len, n_heads * HEAD_DIM), BF16),
        grid=(n_batch, n_heads),
        in_specs=[pl.BlockSpec((ctx_len, HEAD_DIM), lambda b, h: (ctx_row0 + b, h)),
                  pl.BlockSpec((ctx_len, HEAD_DIM), lambda b, h: (ctx_row0 + b, k_col0 + h // group)),
                  pl.BlockSpec((ctx_len, HEAD_DIM), lambda b, h: (ctx_row0 + b, v_col0 + h // group))],
        out_specs=pl.BlockSpec((ctx_len, HEAD_DIM), lambda b, h: (b, h)),
        compiler_params=_params(("arbitrary", "arbitrary"), 8 * ctx_len * HEAD_DIM * 2 + 4 * ctx_len * ctx_len * 4),
        name="attn_ctx",
    )(y, y, y)


def _diff_lambda(lam_ref, lambda_init):
    lq1, lk1, lq2, lk2 = (lam_ref[r:r + 1, :] for r in range(4))
    return (jnp.exp(jnp.sum(lq1 * lk1, axis=-1, keepdims=True))
            - jnp.exp(jnp.sum(lq2 * lk2, axis=-1, keepdims=True)) + lambda_init)


def _diff_combine(o0_t, o1_t, lam, g_ref, lambda_init):
    o = (o0_t - lam * o1_t).T
    o = o * lax.rsqrt(jnp.mean(o * o, axis=-1, keepdims=True) + EPS) * g_ref[...]
    return o * (1.0 - lambda_init)


def _ctx_diff_kernel(lam_ref, g_ref, q_ref, k_ref, v_ref, o_ref, *, lambda_init):
    q, k, v = q_ref[...], k_ref[...], v_ref[...]
    o0 = _ctx_softmax(q[:, :HEAD_DIM], k[:, :HEAD_DIM], v)
    o1 = _ctx_softmax(q[:, HEAD_DIM:], k[:, HEAD_DIM:], v)
    o_ref[...] = _diff_combine(o0, o1, _diff_lambda(lam_ref, lambda_init), g_ref, lambda_init).astype(o_ref.dtype)


def _ctx_diff_attention(y, lam_vecs, subln_g, lambda_init, n_batch, seq, ctx_len, n_heads):
    dh = 2 * HEAD_DIM
    ctx_row0 = n_batch * seq // ctx_len
    return pl.pallas_call(
        functools.partial(_ctx_diff_kernel, lambda_init=lambda_init),
        out_shape=jax.ShapeDtypeStruct((n_batch * ctx_len, n_heads * dh), BF16),
        grid=(n_batch, n_heads),
        in_specs=[pl.BlockSpec((8, HEAD_DIM), lambda b, h: (0, 0)),
                  pl.BlockSpec((1, dh), lambda b, h: (0, 0)),
                  pl.BlockSpec((ctx_len, dh), lambda b, h: (ctx_row0 + b, h)),
                  pl.BlockSpec((ctx_len, dh), lambda b, h: (ctx_row0 + b, n_heads + h)),
                  pl.BlockSpec((ctx_len, dh), lambda b, h: (ctx_row0 + b, 2 * n_heads + h))],
        out_specs=pl.BlockSpec((ctx_len, dh), lambda b, h: (b, h)),
        compiler_params=_params(("arbitrary", "arbitrary"), 8 * ctx_len * dh * 2 + 6 * ctx_len * ctx_len * 4),
        name="diff_ctx",
    )(lam_vecs, subln_g.reshape(1, dh), y, y, y)


PIPE_TQ = 512
PIPE_CHUNK = 256


def _pipe_kernel(*refs, diff, lambda_init):
    if diff:
        lam_ref, g_ref, q_ref, k_ref, v_ref, kc_ref, vc_ref, o_ref, sa_ref, sb_ref, m_ref, o0_ref = refs
    else:
        q_ref, k_ref, v_ref, kc_ref, vc_ref, o_ref, sa_ref, sb_ref, m_ref = refs
    j = pl.program_id(2)
    tq = q_ref.shape[0]
    dv = v_ref.shape[1]

    @pl.when(j == 0)
    def _():
        sb_ref[...] = jnp.zeros(sb_ref.shape, F32)
        m_ref[1] = jnp.zeros(m_ref.shape[1:], F32)
        if diff:
            o0_ref[...] = jnp.zeros(o0_ref.shape, F32)

    def body(s_cur, s_prev, cur, prev):
        q = q_ref[...]
        m_new = jnp.full((1, tq), NEG, F32)
        m_old = m_ref[prev][0:1, :]
        l = jnp.zeros((1, tq), F32)
        acc = jnp.zeros((dv, tq), F32)
        blocks = [(k_ref, v_ref, c * PIPE_CHUNK, PIPE_CHUNK) for c in range(k_ref.shape[0] // PIPE_CHUNK)]
        blocks.append((kc_ref, vc_ref, 0, kc_ref.shape[0]))
        off = 0
        for kr, vr, r0, n in blocks:
            s = lax.dot_general(kr[r0:r0 + n, :], q, (((1,), (1,)), ((), ())), preferred_element_type=F32)
            s_cur[off:off + n, :] = s
            m_new = jnp.maximum(m_new, jnp.max(s, axis=0, keepdims=True))
            p = jnp.exp2(s_prev[off:off + n, :] - m_old)
            l = l + jnp.sum(p, axis=0, keepdims=True)
            acc = acc + lax.dot_general(vr[r0:r0 + n, :], p.astype(BF16), (((0,), (0,)), ((), ())),
                                        preferred_element_type=F32)
            off += n
        m_ref[cur] = jnp.broadcast_to(m_new, m_ref.shape[1:])
        o_t = acc / l
        if not diff:
            o_ref[...] = o_t.T.astype(o_ref.dtype)
        elif cur == 1:
            o0_ref[...] = o_t
        else:
            lam = _diff_lambda(lam_ref, lambda_init)
            o_ref[...] = _diff_combine(o0_ref[...], o_t, lam, g_ref, lambda_init).astype(o_ref.dtype)

    @pl.when(j % 2 == 0)
    def _():
        body(sa_ref, sb_ref, 0, 1)

    @pl.when(j % 2 == 1)
    def _():
        body(sb_ref, sa_ref, 1, 0)


def _pipe_attention(y, d_out, n_batch, seq, ctx_len, n_outer, items_per_qblock, heads_per_outer, dv,
                    q_col, k_col, v_col, out_col, extra_args, extra_specs, lambda_init, name):
    diff = lambda_init is not None
    n_lat = n_batch * seq
    ctx_row0 = n_lat // ctx_len
    tq = PIPE_TQ
    nq = seq // tq
    n_items = nq * items_per_qblock * heads_per_outer
    n_keys = seq + ctx_len

    def qblock(it):
        return (it // items_per_qblock) % nq

    def cur_item(j):
        return jnp.minimum(j, n_items - 1)

    def prev_item(j):
        return jnp.maximum(j - 1, 0)

    in_specs = list(extra_specs) + [
        pl.BlockSpec((tq, HEAD_DIM), lambda b, g, j: (b * nq + qblock(cur_item(j)), q_col(g, cur_item(j)))),
        pl.BlockSpec((seq, HEAD_DIM), lambda b, g, j: (b, k_col(g, cur_item(j)))),
        pl.BlockSpec((seq, dv), lambda b, g, j: (b, v_col(g, prev_item(j)))),
        pl.BlockSpec((ctx_len, HEAD_DIM), lambda b, g, j: (ctx_row0 + b, k_col(g, cur_item(j)))),
        pl.BlockSpec((ctx_len, dv), lambda b, g, j: (ctx_row0 + b, v_col(g, prev_item(j))))]
    scratch = [pltpu.VMEM((n_keys, tq), F32), pltpu.VMEM((n_keys, tq), F32), pltpu.VMEM((2, 8, tq), F32)]
    if diff:
        scratch.append(pltpu.VMEM((dv, tq), F32))
    out_w = dv if diff else HEAD_DIM
    return pl.pallas_call(
        functools.partial(_pipe_kernel, diff=diff, lambda_init=lambda_init),
        out_shape=jax.ShapeDtypeStruct((n_lat, d_out), BF16),
        grid=(n_batch, n_outer, n_items + 1),
        in_specs=in_specs,
        out_specs=pl.BlockSpec((tq, out_w),
                               lambda b, g, j: (b * nq + qblock(prev_item(j)), out_col(g, prev_item(j)))),
        scratch_shapes=scratch,
        compiler_params=_params(("arbitrary", "arbitrary", "arbitrary"),
                                2 * n_keys * tq * 4 + 4 * n_keys * (HEAD_DIM + dv) * 2 + 8 * tq * dv * 4
                                + 8 * PIPE_CHUNK * tq * 4),
        name=name,
    )(*extra_args, y, y, y, y, y)


NB_Q_ROWS = 8
NB_K_ROWS = 16


def _nb_chunks(rows):
    wr = min(WIN_R, rows)
    chunks, patterns = [], []
    for c in range(rows // NB_Q_ROWS):
        kr0 = min(max(NB_Q_ROWS * c - WIN_R // 2, 0), rows - NB_K_ROWS)
        pat = []
        for a in range(NB_K_ROWS):
            for rq in range(NB_Q_ROWS):
                r = NB_Q_ROWS * c + rq
                r0 = min(max(r - WIN_R // 2, 0), rows - wr)
                kr = kr0 + a
                pat.append(kr - r + (WIN_R - 1) if r0 <= kr < r0 + wr else None)
        pat = tuple(pat)
        if pat not in patterns:
            patterns.append(pat)
        chunks.append((kr0, patterns.index(pat)))
    pairs = []
    for pat in patterns:
        for t in range(0, len(pat), 2):
            if pat[t:t + 2] not in pairs:
                pairs.append(pat[t:t + 2])
    return chunks, patterns, pairs


def _nb_kernel(tz_ref, q_ref, k_ref, v_ref, kc_ref, vc_ref, o_ref, bias_ref, *, chunks, patterns, pairs):
    qw = NB_Q_ROWS * GRID_W
    kw = NB_K_ROWS * GRID_W

    @pl.when(pl.program_id(1) == 0)
    def _():
        for p, pat in enumerate(patterns):
            for a in range(NB_K_ROWS):
                for j in range(NB_Q_ROWS // 2):
                    t = a * NB_Q_ROWS + 2 * j
                    bias_ref[p, a * GRID_W:(a + 1) * GRID_W, j * LANES:(j + 1) * LANES] = (
                        tz_ref[0, pairs.index(pat[t:t + 2])])

    kctx = kc_ref[...]
    vctx = vc_ref[...]

    def scores(c):
        kr0, p = chunks[c]
        q = q_ref[c * qw:(c + 1) * qw, :]
        kb = k_ref[kr0 * GRID_W:kr0 * GRID_W + kw, :]
        dims = (((1,), (1,)), ((), ()))
        s_w = lax.dot_general(kb, q, dims, preferred_element_type=F32) + bias_ref[p]
        s_c = lax.dot_general(kctx, q, dims, preferred_element_type=F32)
        return s_w, s_c

    def finish(c, s_w, s_c):
        kr0 = chunks[c][0]
        vb = v_ref[kr0 * GRID_W:kr0 * GRID_W + kw, :]
        m = jnp.maximum(jnp.max(s_w, axis=0, keepdims=True), jnp.max(s_c, axis=0, keepdims=True))
        p_w = jnp.exp2(s_w - m)
        p_c = jnp.exp2(s_c - m)
        l = jnp.sum(p_w, axis=0, keepdims=True) + jnp.sum(p_c, axis=0, keepdims=True)
        tdims = (((0,), (0,)), ((), ()))
        o_t = (lax.dot_general(vb, p_w.astype(BF16), tdims, preferred_element_type=F32)
               + lax.dot_general(vctx, p_c.astype(BF16), tdims, preferred_element_type=F32))
        o_ref[c * qw:(c + 1) * qw, :] = (o_t / l).T.astype(o_ref.dtype)

    pending = scores(0)
    for c in range(1, len(chunks)):
        upcoming = scores(c)
        finish(c - 1, *pending)
        pending = upcoming
    finish(len(chunks) - 1, *pending)


def _nb_bias_tiles(rpb, pairs):
    kc = jnp.arange(GRID_W, dtype=jnp.int32)[:, None]
    qc = jnp.arange(GRID_W, dtype=jnp.int32)[None, :]
    c0 = jnp.clip(qc - WIN_C // 2, 0, GRID_W - WIN_C)
    in_win = (kc >= c0) & (kc < c0 + WIN_C)
    dc = jnp.clip(kc - qc + (WIN_C - 1), 0, 2 * WIN_C - 2)
    tiles = jnp.where(in_win[None, None], rpb[:, :, dc] * LOG2E, NEG)
    neg = jnp.full((rpb.shape[0], GRID_W, GRID_W), NEG, F32)
    half = lambda dr: neg if dr is None else tiles[:, dr]
    return jnp.stack([jnp.concatenate([half(e), half(o)], axis=-1) for e, o in pairs], axis=1)


def _nb_attention(y, rpb, n_batch, seq, ctx_len, n_heads):
    rows = seq // GRID_W
    assert seq % GRID_W == 0 and rows % NB_Q_ROWS == 0 and rows >= NB_K_ROWS
    chunks, patterns, pairs = _nb_chunks(rows)
    tz = _nb_bias_tiles(rpb, pairs)
    n_lat = n_batch * seq
    ctx_row0 = n_lat // ctx_len
    qw = NB_Q_ROWS * GRID_W
    kw = NB_K_ROWS * GRID_W
    kern = functools.partial(_nb_kernel, chunks=tuple(chunks), patterns=tuple(patterns), pairs=tuple(pairs))
    return pl.pallas_call(
        kern,
        out_shape=jax.ShapeDtypeStruct((n_lat, n_heads * HEAD_DIM), BF16),
        grid=(n_heads, n_batch),
        in_specs=[pl.BlockSpec((1, len(pairs), GRID_W, 2 * GRID_W), lambda h, b: (h, 0, 0, 0)),
                  pl.BlockSpec((seq, HEAD_DIM), lambda h, b: (b, h)),
                  pl.BlockSpec((seq, HEAD_DIM), lambda h, b: (b, n_heads + h)),
                  pl.BlockSpec((seq, HEAD_DIM), lambda h, b: (b, 2 * n_heads + h)),
                  pl.BlockSpec((ctx_len, HEAD_DIM), lambda h, b: (ctx_row0 + b, n_heads + h)),
                  pl.BlockSpec((ctx_len, HEAD_DIM), lambda h, b: (ctx_row0 + b, 2 * n_heads + h))],
        out_specs=pl.BlockSpec((seq, HEAD_DIM), lambda h, b: (b, h)),
        scratch_shapes=[pltpu.VMEM((len(patterns), kw, qw), F32)],
        compiler_params=_params(("arbitrary", "arbitrary"),
                                8 * seq * HEAD_DIM * 2 + len(patterns) * kw * qw * 4
                                + 2 * len(pairs) * GRID_W * LANES * 4 + 6 * (kw + ctx_len) * qw * 4),
        name="attn_nb",
    )(tz, y, y, y, y, y)


def _rope_tables(seq, pad_rows):
    t = jnp.arange(seq, dtype=jnp.int32)
    row = (t // GRID_W).astype(F32)
    col = (t % GRID_W).astype(F32)
    n_freq = HEAD_DIM // 4
    inv_freq = ROPE_THETA ** (-jnp.arange(n_freq, dtype=F32) / n_freq)
    ang = jnp.concatenate([row[:, None] * inv_freq, col[:, None] * inv_freq], axis=-1)
    cos, sin = jnp.cos(ang), jnp.sin(ang)
    cos2 = jnp.concatenate([cos, cos], axis=-1)
    sin2 = jnp.concatenate([-sin, sin], axis=-1)
    cos2 = jnp.concatenate([cos2, jnp.ones((pad_rows, HEAD_DIM), F32)], axis=0)
    sin2 = jnp.concatenate([sin2, jnp.zeros((pad_rows, HEAD_DIM), F32)], axis=0)
    return cos2, sin2


def _pair_split_index():
    j = jnp.arange(HEAD_DIM, dtype=jnp.int32)
    half = HEAD_DIM // 2
    return jnp.where(j < half, 2 * j, 2 * (j - half) + 1)


def _cast_split_kernel(w_ref, p_ref, o_ref, *, n_split_blocks):
    w = w_ref[...].astype(BF16)

    @pl.when(pl.program_id(1) < n_split_blocks)
    def _():
        for h0 in range(0, w.shape[1], HEAD_DIM):
            o_ref[:, h0:h0 + HEAD_DIM] = jnp.dot(w[:, h0:h0 + HEAD_DIM], p_ref[...],
                                                 preferred_element_type=F32).astype(BF16)

    @pl.when(pl.program_id(1) >= n_split_blocks)
    def _():
        o_ref[...] = w


def _cast_split_pairs(w, layer, n_cols):
    _, k, n = w.shape
    tk, tn = 1024, 512
    perm = (jnp.arange(HEAD_DIM, dtype=jnp.int32)[:, None] == _pair_split_index()[None, :]).astype(BF16)
    return pl.pallas_call(
        functools.partial(_cast_split_kernel, n_split_blocks=n_cols // tn),
        out_shape=jax.ShapeDtypeStruct((1, k, n), BF16),
        grid=(k // tk, n // tn),
        in_specs=[pl.BlockSpec((None, tk, tn), lambda i, j: (layer, i, j)),
                  pl.BlockSpec((HEAD_DIM, HEAD_DIM), lambda i, j: (0, 0))],
        out_specs=pl.BlockSpec((None, tk, tn), lambda i, j: (0, i, j)),
        compiler_params=_params(("arbitrary", "arbitrary"), 2 * tk * tn * 4 + 4 * tk * tn * 2 + 2 * tk * tn * 4),
        name="cast_split",
    )(w, perm)


def kernel(x, c, ctx, c_ctx, norm1_g, norm2_g, mod_down, mod_up, mod_b, mlp_w1, mlp_w2, a_w_qkv, a_w_o, a_q_g, a_k_g, b_w_qkv, b_w_o, b_rpb, c_w_qkv, c_w_o, c_lam_q1, c_lam_k1, c_lam_q2, c_lam_k2, c_subln_g, final_g):
    n_batch, seq, d = x.shape
    ctx_len = ctx.shape[1]
    depth = norm1_g.shape[0]
    n_lat = n_batch * seq
    n_all = n_lat + n_batch * ctx_len
    n_mixers = 3
    heads = d // HEAD_DIM

    h = jnp.concatenate([x.reshape(n_lat, d), ctx.reshape(n_batch * ctx_len, d)], axis=0)
    cond = jnp.concatenate([c, c_ctx[None], jnp.zeros((MOD_ROWS - n_batch - 1, d), F32)], axis=0)
    mods = _modulation(cond, mod_down, mod_up, mod_b)
    tm = _row_tile(seq, n_batch * ctx_len)
    rope_tabs = _rope_tables(seq, tm)
    ones_g = jnp.ones((1, HEAD_DIM), F32)
    w1_all, w2_all = mlp_w1.astype(BF16), mlp_w2.astype(BF16)
    wo_all = (a_w_o.astype(BF16), b_w_o.astype(BF16), c_w_o.astype(BF16))

    for i in range(depth):
        with_ctx_out = i < depth - 1
        n_rows = n_all if with_ctx_out else n_lat
        mod = mods[i]
        kind, slot = i % n_mixers, i // n_mixers
        u = _norm_mod(h, norm1_g[i], mod, 0, 1, n_all, seq, n_batch, BF16, tm)
        if kind == 0:
            kv_heads = heads // 4
            n_q, n_kv = heads * HEAD_DIM, kv_heads * HEAD_DIM
            split = _pair_split_index()
            y = _mm_qkv(u, _cast_split_pairs(a_w_qkv, slot, n_q + n_kv), 0, rope_tabs,
                        a_q_g[slot][split][None], a_k_g[slot][split][None], "A", n_q, n_kv, seq, n_batch, tm)
            nq = seq // PIPE_TQ
            o = _pipe_attention(y, d, n_batch, seq, ctx_len, kv_heads, 1, 4, HEAD_DIM,
                                lambda g, it: g * 4 + it // nq, lambda g, it: heads + g,
                                lambda g, it: heads + kv_heads + g, lambda g, it: g * 4 + it // nq,
                                (), (), None, "attn_lat")
            if with_ctx_out:
                o_c = _ctx_gqa_attention(y, n_batch, seq, ctx_len, heads, 4, heads, heads + kv_heads)
        elif kind == 1:
            y = _mm_qkv(u, b_w_qkv.astype(BF16), slot, rope_tabs, ones_g, ones_g, "B", d, d, seq, n_batch, tm)
            o = _nb_attention(y, b_rpb[slot], n_batch, seq, ctx_len, heads)
            if with_ctx_out:
                o_c = _ctx_gqa_attention(y, n_batch, seq, ctx_len, heads, 1, heads, 2 * heads)
        else:
            lambda_init = 0.8 - 0.6 * math.exp(-0.3 * i)
            y = _mm_qkv(u, _cast_split_pairs(c_w_qkv, slot, 2 * d), 0, rope_tabs, ones_g, ones_g, "C", d, d,
                        seq, n_batch, tm)
            lam_vecs = jnp.concatenate([c_lam_q1[slot][None], c_lam_k1[slot][None], c_lam_q2[slot][None],
                                        c_lam_k2[slot][None], jnp.zeros((4, HEAD_DIM), F32)], axis=0)
            dh = 2 * HEAD_DIM
            o = _pipe_attention(y, d, n_batch, seq, ctx_len, heads // 2, 2, 1, dh,
                                lambda g, it: 2 * g + it % 2, lambda g, it: heads + 2 * g + it % 2,
                                lambda g, it: heads + g, lambda g, it: g,
                                (lam_vecs, c_subln_g[slot].reshape(1, dh)),
                                (pl.BlockSpec((8, HEAD_DIM), lambda b, g, j: (0, 0)),
                                 pl.BlockSpec((1, dh), lambda b, g, j: (0, 0))),
                                lambda_init, "diff_lat")
            if with_ctx_out:
                o_c = _ctx_diff_attention(y, lam_vecs, c_subln_g[slot], lambda_init, n_batch, seq, ctx_len,
                                          heads // 2)
        h = _mm_resid(o, o_c if with_ctx_out else None, wo_all[kind], slot, h, mod, 2, n_rows, seq, n_batch, tm)
        v = _norm_mod(h, norm2_g[i], mod, 3, 4, n_rows, seq, n_batch, BF16, tm)
        a = _mm_act(v, w1_all, i, n_rows, True, tm)
        h = _mm_resid(a, None, w2_all, i, h, mod, 5, n_rows, seq, n_batch, tm)

    zeros_mod = jnp.zeros((MOD_ROWS, N_MOD * d), F32)
    out = _norm_mod(h, final_g, zeros_mod, 0, 1, n_lat, seq, n_batch, F32, tm)
    return out.reshape(n_batch, seq, d)
```
